```python
import jax
import jax.numpy as jnp
from jax import lax

D_MODEL = 2048
BATCH = 4
SEQ = 2048
DEPTH = 2
DEC_BATCH = 128
DEC_SEQ = 1
PAST_LEN = 2048
PAGE_SIZE = 128

A_HEADS = 8
A_HEAD_DIM = 128
A_WIDTH = A_HEADS * A_HEAD_DIM
DILATED_PATTERNS = ((128, 1), (512, 4), (2048, 16))
W_MAX = 2048
QB = 128
R_HEADS = 4
R_QK_DIM = 256
R_V_DIM = 256
R_WIDTH = R_HEADS * R_V_DIM
R_CHUNK = 128
ROPE_BASE = 10000.0
MIX_WIDTH = A_WIDTH + R_WIDTH
OFF_QA = 0
OFF_KA = A_WIDTH
OFF_VA = 2 * A_WIDTH
OFF_QR = 3 * A_WIDTH
OFF_KR = OFF_QR + R_HEADS * R_QK_DIM
OFF_VR = OFF_KR + R_HEADS * R_QK_DIM
OFF_GR = OFF_VR + R_WIDTH
IN_COLS = OFF_GR + R_WIDTH
N_EXPERTS = 16
N_GROUPS = 4
EXPERTS_PER_GROUP = N_EXPERTS // N_GROUPS
TOP_K = 2
D_EXPERT = 1024
MOE_BLOCK = 128
PLE_DIM = 256
ALPHA = (2 * DEPTH) ** 0.25
BETA = (8 * DEPTH) ** -0.25
LN_EPS = 1e-5

kernel_name = "hymba_dilated_retention_moe_step"


def layer_norm(x, g, b):
    xf = x.astype(jnp.float32)
    mu = xf.mean(-1, keepdims=True)
    var = jnp.square(xf - mu).mean(-1, keepdims=True)
    y = (xf - mu) * lax.rsqrt(var + LN_EPS)
    return (y * g.astype(jnp.float32) + b.astype(jnp.float32)).astype(x.dtype)


def project(x, w_in):
    b, s, _ = x.shape
    h = x @ w_in

    def heads(off, n, d):
        return h[..., off:off + n * d].reshape(b, s, n, d)

    return (heads(OFF_QA, A_HEADS, A_HEAD_DIM), heads(OFF_KA, A_HEADS, A_HEAD_DIM),
            heads(OFF_VA, A_HEADS, A_HEAD_DIM), heads(OFF_QR, R_HEADS, R_QK_DIM),
            heads(OFF_KR, R_HEADS, R_QK_DIM), heads(OFF_VR, R_HEADS, R_V_DIM),
            h[..., OFF_GR:OFF_GR + R_WIDTH])


def dilated_attn_prompt(q, k, v, window, dil):
    b, s, nh, hd = q.shape
    reach = window // dil
    sub = s // dil
    nb = -(-sub // QB)
    pad = nb * QB - sub

    def to_blocks(t):
        t = t.reshape(b, sub, dil, nh, hd).transpose(0, 2, 1, 3, 4)
        t = jnp.pad(t, ((0, 0), (0, 0), (0, pad), (0, 0), (0, 0)))
        return t.reshape(b, dil, nb, QB, nh, hd)

    def with_prev(t):
        prev = jnp.pad(t[:, :, :-1], ((0, 0), (0, 0), (1, 0), (0, 0), (0, 0), (0, 0)))
        return jnp.concatenate([prev, t], axis=3)

    qb = to_blocks(q)
    kb = with_prev(to_blocks(k))
    vb = with_prev(to_blocks(v))
    sc = jnp.einsum('brnqhd,brnkhd->brnhqk', qb, kb).astype(jnp.float32) * A_HEAD_DIM ** -0.5
    qi = jnp.arange(QB)[:, None]
    ki = jnp.arange(2 * QB)[None, :]
    rel = qi + QB - ki
    band = (rel >= 0) & (rel <= reach)
    has_prev = (jnp.arange(nb) > 0)[:, None, None] | (ki >= QB)[None]
    mask = band[None] & has_prev
    sc = jnp.where(mask[None, None, :, None], sc, -jnp.inf)
    lse = jax.nn.logsumexp(sc, axis=-1)
    pr = jnp.exp(sc - lse[..., None]).astype(v.dtype)
    o = jnp.einsum('brnhqk,brnkhd->brnqhd', pr, vb)
    o = o.reshape(b, dil, nb * QB, nh, hd)[:, :, :sub].transpose(0, 2, 1, 3, 4).reshape(b, s, nh, hd)
    lse = lse.transpose(0, 1, 2, 4, 3).reshape(b, dil, nb * QB, nh)[:, :, :sub]
    lse = lse.transpose(0, 2, 1, 3).reshape(b, s, nh)
    return o, lse


def dilated_attn_sample(q, kk, vv, past_rows, window, dil):
    t_new = q.shape[1]
    reach = window // dil
    idx = past_rows + jnp.arange(t_new)[:, None] - dil * jnp.arange(reach + 1)[None, :]
    valid = idx >= 0
    idx = jnp.maximum(idx, 0)
    kg = kk[:, idx]
    vg = vv[:, idx]
    sc = jnp.einsum('bthd,btmhd->bhtm', q, kg).astype(jnp.float32) * A_HEAD_DIM ** -0.5
    sc = jnp.where(valid[None, None], sc, -jnp.inf)
    lse = jax.nn.logsumexp(sc, axis=-1)
    pr = jnp.exp(sc - lse[..., None]).astype(vv.dtype)
    o = jnp.einsum('bhtm,btmhd->bthd', pr, vg)
    return o, lse.transpose(0, 2, 1)


def mix_dilations(outs, lses):
    w = jax.nn.softmax(jnp.stack(lses), axis=0)
    return jnp.einsum('pbsh,pbshd->bshd', w, jnp.stack(outs).astype(jnp.float32))


def retention_log_gamma():
    return jnp.log1p(-jnp.exp2(-5.0 - jnp.arange(R_HEADS, dtype=jnp.float32)))


def rotary(x, pos):
    half = x.shape[-1] // 2
    inv = 1.0 / (ROPE_BASE ** jnp.linspace(0.0, 1.0, half, dtype=jnp.float32))
    ang = pos[:, None] * inv[None, :]
    cos = jnp.cos(ang)[None, :, None, :]
    sin = jnp.sin(ang)[None, :, None, :]
    xf = x.astype(jnp.float32)
    x1, x2 = xf[..., :half], xf[..., half:]
    return jnp.concatenate([x1 * cos - x2 * sin, x1 * sin + x2 * cos], axis=-1)


def retention_chunk(st, q, k, v):
    L = q.shape[1]
    lg = retention_log_gamma()
    i = jnp.arange(L, dtype=jnp.float32)
    rel = i[:, None] - i[None, :]
    dmask = jnp.exp(jnp.where(rel[None] >= 0, rel[None] * lg[:, None, None], -jnp.inf))
    inner = jnp.einsum('bihd,bjhd->bhij', q, k) * dmask[None]
    q_dec = q * jnp.exp((i[:, None] + 1.0) * lg[None, :])[None, :, :, None]
    o = jnp.einsum('bhij,bjhe->bihe', inner, v) + jnp.einsum('bihd,bhde->bihe', q_dec, st)
    k_dec = k * jnp.exp((L - 1.0 - i)[:, None] * lg[None, :])[None, :, :, None]
    st_new = jnp.exp(L * lg)[None, :, None, None] * st + jnp.einsum('bjhd,bjhe->bhde', k_dec, v)
    return o, st_new


def retention_prompt(q, k, v):
    b, s = q.shape[:2]
    nc = s // R_CHUNK

    def chunks(t):
        return t.reshape(b, nc, R_CHUNK, *t.shape[2:]).swapaxes(0, 1)

    def step(st, qkv):
        o, st = retention_chunk(st, *qkv)
        return st, o

    st0 = jnp.zeros((b, R_HEADS, R_QK_DIM, R_V_DIM), jnp.float32)
    st, o = lax.scan(step, st0, (chunks(q), chunks(k), chunks(v)))
    return o.swapaxes(0, 1).reshape(b, s, R_HEADS, R_V_DIM), st


def retention_readout(o, g):
    b, s = o.shape[:2]
    mu = o.mean(-1, keepdims=True)
    var = jnp.square(o - mu).mean(-1, keepdims=True)
    on = ((o - mu) * lax.rsqrt(var + LN_EPS)).reshape(b, s, R_WIDTH)
    return on.astype(g.dtype) * jax.nn.silu(g)


def route(h, w_router, b_router):
    n = h.shape[0]
    logits = (h @ w_router).astype(jnp.float32) + b_router.astype(jnp.float32)
    probs = jax.nn.softmax(logits, axis=-1)
    grp = probs.reshape(n, N_GROUPS, EXPERTS_PER_GROUP)
    g_sel = jnp.argmax(grp.max(-1), axis=-1)
    in_grp = grp[jnp.arange(n), g_sel]
    vals, loc = lax.top_k(in_grp, TOP_K)
    idx = (g_sel[:, None] * EXPERTS_PER_GROUP + loc).astype(jnp.int32)
    wts = vals / vals.sum(-1, keepdims=True)
    return idx, wts


def moe(h, w_router, b_router, w_gate_up, w_down):
    n, d = h.shape
    idx, wts = route(h, w_router, b_router)
    a = n * TOP_K
    e_flat = idx.reshape(a)
    tok = jnp.arange(a) // TOP_K
    order = jnp.argsort(e_flat)
    e_s = e_flat[order]
    tok_s = tok[order]
    w_s = wts.reshape(a)[order]
    counts = jnp.zeros((N_EXPERTS,), jnp.int32).at[e_flat].add(1)
    padded = (counts + MOE_BLOCK - 1) // MOE_BLOCK * MOE_BLOCK
    start = jnp.cumsum(counts) - counts
    pend = jnp.cumsum(padded)
    pstart = pend - padded
    dest = pstart[e_s] + jnp.arange(a) - start[e_s]
    nb = (a + N_EXPERTS * (MOE_BLOCK - 1) + MOE_BLOCK - 1) // MOE_BLOCK
    buf = jnp.zeros((nb * MOE_BLOCK, d), h.dtype).at[dest].set(h[tok_s])
    block_e = jnp.minimum(jnp.sum(jnp.arange(nb)[:, None] * MOE_BLOCK >= pend[None, :], axis=1), N_EXPERTS - 1)

    def expert_block(args):
        xb, e = args
        gu = xb @ w_gate_up[e]
        return (jax.nn.silu(gu[:, :D_EXPERT]) * gu[:, D_EXPERT:]) @ w_down[e]

    yb = lax.map(expert_block, (buf.reshape(nb, MOE_BLOCK, d), block_e))
    y_s = yb.reshape(nb * MOE_BLOCK, d)[dest]
    return jnp.zeros((n, d), h.dtype).at[tok_s].add(y_s * w_s[:, None].astype(y_s.dtype))


def post_mixer(x, h_mix, p, ln1_g, ln1_b, ln2_g, ln2_b, w_router, b_router,
               w_gate_up, w_down, w_ple_proj, w_ple_gate, b_ple_gate):
    h = layer_norm(ALPHA * x + h_mix, ln1_g, ln1_b)
    b, s, d = h.shape
    ff = moe(h.reshape(b * s, d), w_router, b_router, w_gate_up, w_down).reshape(b, s, d)
    h = layer_norm(ALPHA * h + ff, ln2_g, ln2_b)
    return h + jax.nn.sigmoid(h @ w_ple_gate + b_ple_gate) * (p @ w_ple_proj)


def prompt_layer(x, p, w_in, w_out, ln1_g, ln1_b, ln2_g, ln2_b, w_router, b_router,
                 w_gate_up, w_down, w_ple_proj, w_ple_gate, b_ple_gate):
    b, s, _ = x.shape
    qa, ka, va, qr, kr, vr, gr = project(x, w_in)
    outs, lses = [], []
    for window, dil in DILATED_PATTERNS:
        o, l = dilated_attn_prompt(qa, ka, va, window, dil)
        outs.append(o)
        lses.append(l)
    attn = mix_dilations(outs, lses).astype(x.dtype).reshape(b, s, A_WIDTH)
    pos = jnp.arange(s, dtype=jnp.float32)
    o_r, st = retention_prompt(rotary(qr, pos), rotary(kr, pos) * R_QK_DIM ** -0.5, vr.astype(jnp.float32))
    ret = retention_readout(o_r, gr)
    h_mix = jnp.concatenate([attn, ret], axis=-1) @ w_out
    y = post_mixer(x, h_mix, p, ln1_g, ln1_b, ln2_g, ln2_b, w_router, b_router,
                   w_gate_up, w_down, w_ple_proj, w_ple_gate, b_ple_gate)
    keep = min(W_MAX, s)
    return y, ka[:, s - keep:], va[:, s - keep:], st.astype(x.dtype)


def sample_layer(x, p, ck, cv, sr, w_in, w_out, ln1_g, ln1_b, ln2_g, ln2_b, w_router, b_router,
                 w_gate_up, w_down, w_ple_proj, w_ple_gate, b_ple_gate):
    b, t, _ = x.shape
    past_rows = ck.shape[1]
    qa, ka, va, qr, kr, vr, gr = project(x, w_in)
    kk = jnp.concatenate([ck, ka.astype(ck.dtype)], axis=1)
    vv = jnp.concatenate([cv, va.astype(cv.dtype)], axis=1)
    outs, lses = [], []
    for window, dil in DILATED_PATTERNS:
        o, l = dilated_attn_sample(qa, kk, vv, past_rows, window, dil)
        outs.append(o)
        lses.append(l)
    attn = mix_dilations(outs, lses).astype(x.dtype).reshape(b, t, A_WIDTH)
    pos = PAST_LEN + jnp.arange(t, dtype=jnp.float32)
    o_r, st = retention_chunk(sr.astype(jnp.float32), rotary(qr, pos),
                              rotary(kr, pos) * R_QK_DIM ** -0.5, vr.astype(jnp.float32))
    ret = retention_readout(o_r, gr)
    h_mix = jnp.concatenate([attn, ret], axis=-1) @ w_out
    y = post_mixer(x, h_mix, p, ln1_g, ln1_b, ln2_g, ln2_b, w_router, b_router,
                   w_gate_up, w_down, w_ple_proj, w_ple_gate, b_ple_gate)
    keep = min(W_MAX, past_rows + t)
    return y, kk[:, past_rows + t - keep:], vv[:, past_rows + t - keep:], st.astype(sr.dtype)


def setup_inputs(seed: int = 0) -> dict:
    key = jax.random.key(seed)
    ks = jax.random.split(key, 24)
    nrm = jax.random.normal
    wbuf = min(W_MAX, PAST_LEN)
    col_scale = (jnp.ones((IN_COLS,), jnp.float32)
                 .at[OFF_VA:OFF_VA + A_WIDTH].set(BETA)
                 .at[OFF_VR:OFF_VR + R_WIDTH].set(BETA))
    return {
        "x_prompt": nrm(ks[0], (BATCH, SEQ, D_MODEL), jnp.float32),
        "x_sample": nrm(ks[1], (DEC_BATCH, DEC_SEQ, D_MODEL), jnp.float32),
        "cache_win_k": nrm(ks[2], (DEPTH, DEC_BATCH, wbuf, A_HEADS, A_HEAD_DIM), jnp.float32),
        "cache_win_v": nrm(ks[3], (DEPTH, DEC_BATCH, wbuf, A_HEADS, A_HEAD_DIM), jnp.float32) * BETA,
        "state_ret": nrm(ks[4], (DEPTH, DEC_BATCH, R_HEADS, R_QK_DIM, R_V_DIM), jnp.float32) * 0.05,
        "p_prompt": nrm(ks[5], (DEPTH, BATCH, SEQ, PLE_DIM), jnp.float32),
        "p_sample": nrm(ks[6], (DEPTH, DEC_BATCH, DEC_SEQ, PLE_DIM), jnp.float32),
        "w_in": nrm(ks[7], (DEPTH, D_MODEL, IN_COLS), jnp.float32) * D_MODEL ** -0.5 * col_scale,
        "w_out": nrm(ks[8], (DEPTH, MIX_WIDTH, D_MODEL), jnp.float32) * MIX_WIDTH ** -0.5 * BETA,
        "ln1_g": 1.0 + 0.02 * nrm(ks[9], (DEPTH, D_MODEL), jnp.float32),
        "ln1_b": 0.02 * nrm(ks[10], (DEPTH, D_MODEL), jnp.float32),
        "ln2_g": 1.0 + 0.02 * nrm(ks[11], (DEPTH, D_MODEL), jnp.float32),
        "ln2_b": 0.02 * nrm(ks[12], (DEPTH, D_MODEL), jnp.float32),
        "w_router": nrm(ks[13], (D_MODEL, N_EXPERTS), jnp.float32) * D_MODEL ** -0.5,
        "b_router": 0.01 * nrm(ks[14], (N_EXPERTS,), jnp.float32),
        "w_gate_up": nrm(ks[15], (DEPTH, N_EXPERTS, D_MODEL, 2 * D_EXPERT), jnp.float32) * D_MODEL ** -0.5,
        "w_down": nrm(ks[16], (DEPTH, N_EXPERTS, D_EXPERT, D_MODEL), jnp.float32) * D_EXPERT ** -0.5 * BETA,
        "w_ple_proj": nrm(ks[17], (DEPTH, PLE_DIM, D_MODEL), jnp.float32) * PLE_DIM ** -0.5,
        "w_ple_gate": nrm(ks[18], (DEPTH, D_MODEL, D_MODEL), jnp.float32) * D_MODEL ** -0.5,
        "b_ple_gate": 0.02 * nrm(ks[19], (DEPTH, D_MODEL), jnp.float32),
    }


def reference(x_prompt, x_sample, cache_win_k, cache_win_v, state_ret, p_prompt, p_sample,
              w_in, w_out, ln1_g, ln1_b, ln2_g, ln2_b, w_router, b_router,
              w_gate_up, w_down, w_ple_proj, w_ple_gate, b_ple_gate):
    xp, xs = x_prompt, x_sample
    pk, pv, ps, sk, sv, ss = [], [], [], [], [], []
    for i in range(DEPTH):
        xp, k_p, v_p, s_p = prompt_layer(
            xp, p_prompt[i], w_in[i], w_out[i], ln1_g[i], ln1_b[i], ln2_g[i], ln2_b[i],
            w_router, b_router, w_gate_up[i], w_down[i], w_ple_proj[i], w_ple_gate[i], b_ple_gate[i])
        xs, k_s, v_s, s_s = sample_layer(
            xs, p_sample[i], cache_win_k[i], cache_win_v[i], state_ret[i],
            w_in[i], w_out[i], ln1_g[i], ln1_b[i], ln2_g[i], ln2_b[i],
            w_router, b_router, w_gate_up[i], w_down[i], w_ple_proj[i], w_ple_gate[i], b_ple_gate[i])
        pk.append(k_p)
        pv.append(v_p)
        ps.append(s_p)
        sk.append(k_s)
        sv.append(v_s)
        ss.append(s_s)
    return (xp, xs, jnp.stack(pk), jnp.stack(pv), jnp.stack(ps), jnp.stack(sk), jnp.stack(sv), jnp.stack(ss))
```

```python
import functools

import jax
import jax.numpy as jnp
from jax import lax
from jax.experimental import pallas as pl
from jax.experimental.pallas import tpu as pltpu

F32 = jnp.float32
BF16 = jnp.bfloat16

D_MODEL = 2048
BATCH = 4
SEQ = 2048
DEPTH = 2
DEC_BATCH = 128
PAST_LEN = 2048
N_PROMPT = BATCH * SEQ
N_TOK = N_PROMPT + DEC_BATCH

A_HEADS = 8
A_HEAD_DIM = 128
A_WIDTH = A_HEADS * A_HEAD_DIM
DILATED_PATTERNS = ((128, 1), (512, 4), (2048, 16))
W_MAX = 2048
QB = 128
R_HEADS = 4
R_QK_DIM = 256
R_V_DIM = 256
R_WIDTH = R_HEADS * R_V_DIM
R_CHUNK = 128
ROPE_BASE = 10000.0
OFF_QA = 0
OFF_KA = A_WIDTH
OFF_VA = 2 * A_WIDTH
OFF_QR = 3 * A_WIDTH
OFF_KR = OFF_QR + R_HEADS * R_QK_DIM
OFF_VR = OFF_KR + R_HEADS * R_QK_DIM
OFF_GR = OFF_VR + R_WIDTH
IN_COLS = OFF_GR + R_WIDTH
N_EXPERTS = 16
N_GROUPS = 4
EXPERTS_PER_GROUP = N_EXPERTS // N_GROUPS
TOP_K = 2
D_EXPERT = 1024
PLE_DIM = 256
ALPHA = (2 * DEPTH) ** 0.25
LN_EPS = 1e-5
NEG = -1e30

LANES = 128
MIB = 1024 * 1024

INPROJ_TM = 1664
INPROJ_TN = 512
ROW_TM = 320
MOE_RB = 256
N_ASSIGN = N_TOK * TOP_K
MOE_NB = (N_ASSIGN + N_EXPERTS * (MOE_RB - 1) + MOE_RB - 1) // MOE_RB
SAMPLE_CH = 512


def _cparams(sem, vmem_mib):
    return pltpu.CompilerParams(dimension_semantics=sem, vmem_limit_bytes=vmem_mib * MIB)


def _dot(a, b):
    return jnp.dot(a, b, preferred_element_type=F32)


def _dot_nt(a, b):
    return lax.dot_general(a, b, (((1,), (1,)), ((), ())), preferred_element_type=F32)


def _dot_tn(a, b):
    return lax.dot_general(a, b, (((0,), (0,)), ((), ())), preferred_element_type=F32)


def _inproj_body(x_ref, w_ref, o_ref):
    o_ref[...] = _dot(x_ref[...], w_ref[...].astype(BF16))


def _inproj(xb, w_in, layer):
    return pl.pallas_call(
        _inproj_body,
        grid=(N_TOK // INPROJ_TM, IN_COLS // INPROJ_TN),
        in_specs=[
            pl.BlockSpec((INPROJ_TM, D_MODEL), lambda i, j: (i, 0)),
            pl.BlockSpec((None, D_MODEL, INPROJ_TN), lambda i, j: (layer, 0, j)),
        ],
        out_specs=pl.BlockSpec((INPROJ_TM, INPROJ_TN), lambda i, j: (i, j)),
        out_shape=jax.ShapeDtypeStruct((N_TOK, IN_COLS), F32),
        compiler_params=_cparams(("parallel", "arbitrary"), 48),
        name="inproj",
    )(xb, w_in)


def _attn_prompt_body(q_ref, k_ref, v_ref, o_ref, o_scr, lse_scr):
    scale = A_HEAD_DIM ** -0.5
    qi = lax.broadcasted_iota(jnp.int32, (QB, QB), 0)
    ki = lax.broadcasted_iota(jnp.int32, (QB, QB), 1)
    cur_ok = qi >= ki
    prev_ok = ki >= qi

    for p, (window, dil) in enumerate(DILATED_PATTERNS):
        assert window // dil == QB
        nb = (SEQ // dil) // QB

        def body(it, carry, p=p, dil=dil, nb=nb):
            r = it // nb
            n = it % nb
            start = r + n * (QB * dil)
            rows = pl.ds(start, QB, stride=dil) if dil > 1 else pl.ds(pl.multiple_of(start, QB), QB)
            q = q_ref[rows, :].astype(BF16)
            kc = k_ref[rows, :].astype(BF16)
            vc = v_ref[rows, :].astype(BF16)
            sc = jnp.where(cur_ok, _dot_nt(q, kc) * scale, NEG)
            m = jnp.max(sc, axis=-1, keepdims=True)
            if nb > 1:
                pstart = r + jnp.maximum(n - 1, 0) * (QB * dil)
                prow = (pl.ds(pstart, QB, stride=dil) if dil > 1
                        else pl.ds(pl.multiple_of(pstart, QB), QB))
                kp = k_ref[prow, :].astype(BF16)
                vp = v_ref[prow, :].astype(BF16)
                sp = jnp.where(prev_ok & (n > 0), _dot_nt(q, kp) * scale, NEG)
                m = jnp.maximum(m, jnp.max(sp, axis=-1, keepdims=True))
                pp = jnp.exp(sp - m)
            pc = jnp.exp(sc - m)
            l = jnp.sum(pc, axis=-1, keepdims=True)
            acc = _dot(pc.astype(BF16), vc)
            if nb > 1:
                l = l + jnp.sum(pp, axis=-1, keepdims=True)
                acc = acc + _dot(pp.astype(BF16), vp)
            o_scr[p, rows, :] = acc / l
            lse_scr[p, rows, :] = jnp.broadcast_to(m + jnp.log(l), (QB, LANES))
            return carry

        lax.fori_loop(0, dil * nb, body, 0)

    def mix(c, carry):
        rows = pl.ds(pl.multiple_of(c * QB, QB), QB)
        l0 = lse_scr[0, rows, :]
        l1 = lse_scr[1, rows, :]
        l2 = lse_scr[2, rows, :]
        mx = jnp.maximum(jnp.maximum(l0, l1), l2)
        w0 = jnp.exp(l0 - mx)
        w1 = jnp.exp(l1 - mx)
        w2 = jnp.exp(l2 - mx)
        num = w0 * o_scr[0, rows, :] + w1 * o_scr[1, rows, :] + w2 * o_scr[2, rows, :]
        o_ref[rows, :] = (num / (w0 + w1 + w2)).astype(o_ref.dtype)
        return carry

    lax.fori_loop(0, SEQ // QB, mix, 0)


def _attn_prompt(h):
    col = lambda off: (lambda b, hh: (b, off // A_HEAD_DIM + hh))
    return pl.pallas_call(
        _attn_prompt_body,
        grid=(BATCH, A_HEADS),
        in_specs=[
            pl.BlockSpec((SEQ, A_HEAD_DIM), col(OFF_QA)),
            pl.BlockSpec((SEQ, A_HEAD_DIM), col(OFF_KA)),
            pl.BlockSpec((SEQ, A_HEAD_DIM), col(OFF_VA)),
        ],
        out_specs=pl.BlockSpec((SEQ, A_HEAD_DIM), lambda b, hh: (b, hh)),
        out_shape=jax.ShapeDtypeStruct((N_PROMPT, A_WIDTH), BF16),
        scratch_shapes=[
            pltpu.VMEM((len(DILATED_PATTERNS), SEQ, A_HEAD_DIM), F32),
            pltpu.VMEM((len(DILATED_PATTERNS), SEQ, LANES), F32),
        ],
        compiler_params=_cparams(("parallel", "parallel"), 32),
        name="attn_prompt",
    )(h, h, h)


def _rotate(x, cos, sin):
    half = x.shape[-1] // 2
    x1 = x[:, :half]
    x2 = x[:, half:]
    return jnp.concatenate([x1 * cos - x2 * sin, x1 * sin + x2 * cos], axis=-1)


def _group_norm_gate(o, g):
    mu = jnp.mean(o, axis=-1, keepdims=True)
    var = jnp.mean(jnp.square(o - mu), axis=-1, keepdims=True)
    on = (o - mu) * lax.rsqrt(var + LN_EPS)
    return on * (g * jax.nn.sigmoid(g))


def _ret_prompt_body(q_ref, k_ref, v_ref, g_ref, cos_ref, sin_ref, dmask_ref, qdec_ref, kdec_ref,
                     cdec_ref, o_ref, st_ref, st_scr):
    st_scr[...] = jnp.zeros_like(st_scr)
    dmask = dmask_ref[...]
    qdec = qdec_ref[...]
    kdec = kdec_ref[...]
    cdec = cdec_ref[...]

    def chunk(c, carry):
        rows = pl.ds(pl.multiple_of(c * R_CHUNK, R_CHUNK), R_CHUNK)
        cos = cos_ref[rows, :]
        sin = sin_ref[rows, :]
        q = _rotate(q_ref[rows, :], cos, sin)
        k = _rotate(k_ref[rows, :], cos, sin) * (R_QK_DIM ** -0.5)
        v = v_ref[rows, :].astype(BF16)
        st = st_scr[...]
        inner = _dot_nt(q.astype(BF16), k.astype(BF16)) * dmask
        o = _dot(inner.astype(BF16), v) + _dot((q * qdec).astype(BF16), st.astype(BF16))
        st_scr[...] = cdec * st + _dot_tn((k * kdec).astype(BF16), v)
        o_ref[rows, :] = _group_norm_gate(o, g_ref[rows, :]).astype(o_ref.dtype)
        return carry

    lax.fori_loop(0, SEQ // R_CHUNK, chunk, 0)
    st_ref[...] = st_scr[...]


def _retention_tables():
    lg = jnp.log1p(-jnp.exp2(-5.0 - jnp.arange(R_HEADS, dtype=F32)))
    i = jnp.arange(R_CHUNK, dtype=F32)
    rel = i[:, None] - i[None, :]
    dmask = jnp.exp(jnp.where(rel[None] >= 0, rel[None] * lg[:, None, None], -jnp.inf))
    qdec = jnp.exp((i[None, :] + 1.0) * lg[:, None])
    kdec = jnp.exp((R_CHUNK - 1.0 - i)[None, :] * lg[:, None])
    cdec = jnp.exp(R_CHUNK * lg)
    bcast = lambda t: jnp.broadcast_to(t[:, :, None], (R_HEADS, R_CHUNK, R_QK_DIM))
    return (dmask, bcast(qdec), bcast(kdec),
            jnp.broadcast_to(cdec[:, None, None], (R_HEADS, R_QK_DIM, R_V_DIM)), lg)


def _rope_tables(pos):
    half = R_QK_DIM // 2
    inv = 1.0 / (ROPE_BASE ** jnp.linspace(0.0, 1.0, half, dtype=F32))
    ang = pos[:, None] * inv[None, :]
    return jnp.cos(ang), jnp.sin(ang)


def _ret_prompt(h, tables, rope):
    dmask, qdec, kdec, cdec, _ = tables
    cos, sin = rope
    col = lambda off: (lambda b, hh: (b, off // R_QK_DIM + hh))
    per_head = lambda shape: pl.BlockSpec((None,) + shape, lambda b, hh: (hh, 0, 0))
    return pl.pallas_call(
        _ret_prompt_body,
        grid=(BATCH, R_HEADS),
        in_specs=[
            pl.BlockSpec((SEQ, R_QK_DIM), col(OFF_QR)),
            pl.BlockSpec((SEQ, R_QK_DIM), col(OFF_KR)),
            pl.BlockSpec((SEQ, R_V_DIM), col(OFF_VR)),
            pl.BlockSpec((SEQ, R_V_DIM), col(OFF_GR)),
            pl.BlockSpec((SEQ, R_QK_DIM // 2), lambda b, hh: (0, 0)),
            pl.BlockSpec((SEQ, R_QK_DIM // 2), lambda b, hh: (0, 0)),
            per_head((R_CHUNK, R_CHUNK)),
            per_head((R_CHUNK, R_QK_DIM)),
            per_head((R_CHUNK, R_QK_DIM)),
            per_head((R_QK_DIM, R_V_DIM)),
        ],
        out_specs=[
            pl.BlockSpec((SEQ, R_V_DIM), lambda b, hh: (b, hh)),
            pl.BlockSpec((None, None, R_QK_DIM, R_V_DIM), lambda b, hh: (b, hh, 0, 0)),
        ],
        out_shape=[
            jax.ShapeDtypeStruct((N_PROMPT, R_WIDTH), BF16),
            jax.ShapeDtypeStruct((BATCH, R_HEADS, R_QK_DIM, R_V_DIM), F32),
        ],
        scratch_shapes=[pltpu.VMEM((R_QK_DIM, R_V_DIM), F32)],
        compiler_params=_cparams(("parallel", "parallel"), 48),
        name="ret_prompt",
    )(h, h, h, h, cos, sin, dmask, qdec, kdec, cdec)


def _sample_attn_body(*refs, n_alias):
    q_ref, kn_ref, vn_ref, ck_ref, ckn_ref, cv_ref, cvn_ref = refs[:7]
    o_ref, ok_ref, ov_ref, m_scr, l_scr, acc_scr = refs[7 + n_alias:]
    c = pl.program_id(1)
    last = pl.num_programs(1) - 1
    scale = A_HEAD_DIM ** -0.5
    q = q_ref[...]

    @pl.when(c == 0)
    def _():
        m_scr[...] = jnp.full_like(m_scr, NEG)
        l_scr[...] = jnp.zeros_like(l_scr)
        acc_scr[...] = jnp.zeros_like(acc_scr)

    def accumulate(kr, vr, mult):
        s = jnp.sum(kr * q[None], axis=-1, keepdims=True) * scale
        m_old = m_scr[...]
        m_new = jnp.maximum(m_old, jnp.max(s, axis=0))
        alpha = jnp.exp(m_old - m_new)
        pr = jnp.exp(s - m_new[None])
        l_scr[...] = alpha * l_scr[...] + mult * jnp.sum(pr, axis=0)
        acc_scr[...] = alpha * acc_scr[...] + mult * jnp.sum(pr * vr, axis=0)
        m_scr[...] = m_new

    accumulate(ck_ref[pl.ds(0, SAMPLE_CH // 16, stride=16)], cv_ref[pl.ds(0, SAMPLE_CH // 16, stride=16)], 1.0)

    @pl.when(c == last)
    def _():
        accumulate(ck_ref[pl.ds(0, QB, stride=4)], cv_ref[pl.ds(0, QB, stride=4)], 1.0)
        accumulate(ck_ref[pl.ds(SAMPLE_CH - QB, QB)], cv_ref[pl.ds(SAMPLE_CH - QB, QB)], 1.0)
        accumulate(kn_ref[...][None], vn_ref[...][None], float(len(DILATED_PATTERNS)))
        o_ref[...] = acc_scr[...] / l_scr[...]

    ok_ref[pl.ds(0, SAMPLE_CH - 1)] = ck_ref[pl.ds(1, SAMPLE_CH - 1)]
    ov_ref[pl.ds(0, SAMPLE_CH - 1)] = cv_ref[pl.ds(1, SAMPLE_CH - 1)]
    is_last = c == last
    ok_ref[SAMPLE_CH - 1] = jnp.where(is_last, kn_ref[...], ckn_ref[0])
    ov_ref[SAMPLE_CH - 1] = jnp.where(is_last, vn_ref[...], cvn_ref[0])


def _sample_attn(q_s, k_s, v_s, cache_k, cache_v, layer, prev_k=None, prev_v=None):
    assert PAST_LEN == W_MAX and W_MAX % SAMPLE_CH == 0 and SAMPLE_CH == 512
    nc = W_MAX // SAMPLE_CH
    row = pl.BlockSpec((None, A_HEADS, A_HEAD_DIM), lambda b, c: (b, 0, 0))
    chunk = pl.BlockSpec((None, None, SAMPLE_CH, A_HEADS, A_HEAD_DIM), lambda b, c: (layer, b, c, 0, 0))
    nxt = pl.BlockSpec((None, None, 1, A_HEADS, A_HEAD_DIM),
                       lambda b, c: (layer, b, jnp.minimum((c + 1) * SAMPLE_CH, W_MAX - 1), 0, 0))
    n_alias = 0 if prev_k is None else 2
    alias_specs = [pl.BlockSpec(memory_space=pl.ANY)] * n_alias
    alias_args = [] if prev_k is None else [prev_k, prev_v]
    win_shape = jax.ShapeDtypeStruct((DEPTH, DEC_BATCH, W_MAX, A_HEADS, A_HEAD_DIM), F32)
    return pl.pallas_call(
        functools.partial(_sample_attn_body, n_alias=n_alias),
        grid=(DEC_BATCH, nc),
        in_specs=[row, row, row, chunk, nxt, chunk, nxt] + alias_specs,
        out_specs=[row, chunk, chunk],
        out_shape=[jax.ShapeDtypeStruct((DEC_BATCH, A_HEADS, A_HEAD_DIM), F32), win_shape, win_shape],
        scratch_shapes=[pltpu.VMEM((A_HEADS, A_HEAD_DIM), F32)] * 3,
        input_output_aliases={7: 1, 8: 2} if n_alias else {},
        compiler_params=_cparams(("parallel", "arbitrary"), 40),
        name="sample_attn",
    )(q_s, k_s, v_s, cache_k, cache_k, cache_v, cache_v, *alias_args)


def _sample_ret_body(*refs, n_alias, gammas):
    q_ref, k_ref, v_ref, g_ref, cos_ref, sin_ref, eye_ref, st_ref = refs[:8]
    o_ref, sto_ref = refs[8 + n_alias:]
    cos = cos_ref[...]
    sin = sin_ref[...]
    eye = eye_ref[...]
    outs = []
    for hh in range(R_HEADS):
        cols = slice(hh * R_QK_DIM, (hh + 1) * R_QK_DIM)
        q = _rotate(q_ref[:, cols], cos, sin)
        k = _rotate(k_ref[:, cols], cos, sin) * (R_QK_DIM ** -0.5)
        v = v_ref[:, cols]
        st = st_ref[hh]
        inner = jnp.sum(q * k, axis=-1, keepdims=True)
        q_dec = jnp.broadcast_to(q * gammas[hh], (8, R_QK_DIM)).astype(BF16)
        o = inner * v + _dot(q_dec, st.astype(BF16))[0:1]
        k_col = _dot_nt(eye, jnp.broadcast_to(k, (R_QK_DIM, R_QK_DIM)).astype(BF16))
        sto_ref[hh] = gammas[hh] * st + k_col * v
        outs.append(_group_norm_gate(o, g_ref[:, cols]))
    o_ref[...] = jnp.concatenate(outs, axis=-1)


def _sample_ret(q_s, k_s, v_s, g_s, state, layer, rope_s, eye, gammas, prev=None):
    row = lambda w: pl.BlockSpec((None, 1, w), lambda b: (b, 0, 0))
    const = lambda shape: pl.BlockSpec(shape, lambda b: (0,) * len(shape))
    st_spec = pl.BlockSpec((None, None, R_HEADS, R_QK_DIM, R_V_DIM), lambda b: (layer, b, 0, 0, 0))
    n_alias = 0 if prev is None else 1
    return pl.pallas_call(
        functools.partial(_sample_ret_body, n_alias=n_alias, gammas=gammas),
        grid=(DEC_BATCH,),
        in_specs=[row(R_HEADS * R_QK_DIM), row(R_HEADS * R_QK_DIM), row(R_WIDTH), row(R_WIDTH),
                  const((1, R_QK_DIM // 2)), const((1, R_QK_DIM // 2)), const((R_QK_DIM, R_QK_DIM)), st_spec]
        + [pl.BlockSpec(memory_space=pl.ANY)] * n_alias,
        out_specs=[row(R_WIDTH), st_spec],
        out_shape=[jax.ShapeDtypeStruct((DEC_BATCH, 1, R_WIDTH), F32),
                   jax.ShapeDtypeStruct((DEPTH, DEC_BATCH, R_HEADS, R_QK_DIM, R_V_DIM), F32)],
        input_output_aliases={8: 1} if n_alias else {},
        compiler_params=_cparams(("parallel",), 32),
        name="sample_ret",
    )(q_s, k_s, v_s, g_s, rope_s[0], rope_s[1], eye, state, *([] if prev is None else [prev]))


def _layer_norm(y, g, b):
    mu = jnp.mean(y, axis=-1, keepdims=True)
    var = jnp.mean(jnp.square(y - mu), axis=-1, keepdims=True)
    return (y - mu) * lax.rsqrt(var + LN_EPS) * g + b


def _split_bf16(x):
    hi = x.astype(BF16)
    lo = (x - hi.astype(F32)).astype(BF16)
    return hi, lo


def _outproj_body(attn_ref, ret_ref, x_ref, w_ref, g_ref, b_ref, wr_ref, br_ref, h_ref, ridx_ref, rw_ref):
    acc = _dot(attn_ref[...], w_ref[0:A_WIDTH, :]) + _dot(ret_ref[...], w_ref[A_WIDTH:, :])
    h = _layer_norm(ALPHA * x_ref[...] + acc, g_ref[...], b_ref[...])
    h_ref[...] = h

    h_hi, h_lo = _split_bf16(h)
    w_hi, w_lo = _split_bf16(wr_ref[...])
    logits = _dot(h_hi, w_hi) + _dot(h_hi, w_lo) + _dot(h_lo, w_hi) + br_ref[...]
    lane = lax.broadcasted_iota(jnp.int32, logits.shape, 1)
    valid = lane < N_EXPERTS
    logits = jnp.where(valid, logits, NEG)
    e = jnp.exp(logits - jnp.max(logits, axis=-1, keepdims=True))
    probs = e / jnp.sum(e, axis=-1, keepdims=True)
    grp = lane // EXPERTS_PER_GROUP
    best = jnp.max(jnp.where(grp == 0, probs, -1.0), axis=-1, keepdims=True)
    sel = jnp.zeros_like(best, dtype=jnp.int32)
    for gi in range(1, N_GROUPS):
        gm = jnp.max(jnp.where(grp == gi, probs, -1.0), axis=-1, keepdims=True)
        better = gm > best
        sel = jnp.where(better, gi, sel)
        best = jnp.maximum(best, gm)
    pv = jnp.where(grp == sel, probs, -1.0)
    v1 = jnp.max(pv, axis=-1, keepdims=True)
    i1 = jnp.min(jnp.where(pv == v1, lane, LANES), axis=-1, keepdims=True)
    pv2 = jnp.where(lane == i1, -1.0, pv)
    v2 = jnp.max(pv2, axis=-1, keepdims=True)
    i2 = jnp.min(jnp.where(pv2 == v2, lane, LANES), axis=-1, keepdims=True)
    tot = v1 + v2
    ridx_ref[...] = jnp.where(lane == 0, i1, jnp.where(lane == 1, i2, 0))
    rw_ref[...] = jnp.where(lane == 0, v1 / tot, jnp.where(lane == 1, v2 / tot, 0.0))


def _outproj(attn, ret, x, w_out_b, ln_g, ln_b, w_router_p, b_router_p, layer):
    rows = lambda w: pl.BlockSpec((ROW_TM, w), lambda i: (i, 0))
    vec = lambda: pl.BlockSpec((None, 1, D_MODEL), lambda i: (layer, 0, 0))
    return pl.pallas_call(
        _outproj_body,
        grid=(N_TOK // ROW_TM,),
        in_specs=[rows(A_WIDTH), rows(R_WIDTH), rows(D_MODEL),
                  pl.BlockSpec((None, A_WIDTH + R_WIDTH, D_MODEL), lambda i: (layer, 0, 0)),
                  vec(), vec(),
                  pl.BlockSpec((D_MODEL, LANES), lambda i: (0, 0)),
                  pl.BlockSpec((1, LANES), lambda i: (0, 0))],
        out_specs=[rows(D_MODEL), rows(LANES), rows(LANES)],
        out_shape=[jax.ShapeDtypeStruct((N_TOK, D_MODEL), F32),
                   jax.ShapeDtypeStruct((N_TOK, LANES), jnp.int32),
                   jax.ShapeDtypeStruct((N_TOK, LANES), F32)],
        compiler_params=_cparams(("parallel",), 48),
        name="outproj_ln_router",
    )(attn, ret, x, w_out_b, ln_g, ln_b, w_router_p, b_router_p)


def _dispatch_plan(ridx):
    e_flat = ridx[:, :TOP_K].reshape(N_ASSIGN)
    onehot = (e_flat[:, None] == jnp.arange(N_EXPERTS, dtype=jnp.int32)[None, :]).astype(jnp.int32)
    csum = jnp.cumsum(onehot, axis=0)
    counts = csum[-1]
    rank = jnp.sum(onehot * csum, axis=1) - 1
    padded = (counts + MOE_RB - 1) // MOE_RB * MOE_RB
    pend = jnp.cumsum(padded)
    pstart = pend - padded
    dest = jnp.sum(onehot * pstart[None, :], axis=1) + rank
    a = jnp.arange(N_ASSIGN, dtype=jnp.int32)
    tok = a // TOP_K
    slot = a % TOP_K
    n_rows = MOE_NB * MOE_RB
    row_tok = jnp.zeros((n_rows,), jnp.int32).at[dest].set(tok)
    is_real = jnp.zeros((n_rows,), jnp.int32).at[dest].set(1)
    spill = N_ASSIGN + jnp.cumsum(1 - is_real) - 1
    row_dst = jnp.where(is_real == 1, jnp.zeros((n_rows,), jnp.int32).at[dest].set(slot * N_TOK + tok), spill)
    blk = jnp.arange(MOE_NB, dtype=jnp.int32) * MOE_RB
    block_e = jnp.minimum(jnp.sum(blk[:, None] >= pend[None, :], axis=1), N_EXPERTS - 1).astype(jnp.int32)
    n_used = (pend[-1] // MOE_RB).astype(jnp.int32).reshape(1)
    return block_e, n_used, row_tok.astype(jnp.int32), row_dst.astype(jnp.int32)


def _moe_body(be_ref, nu_ref, tok_ref, dst_ref, h_hbm, wgu_ref, wd_ref, ff_hbm, xbuf, ybuf, gsem, ssem):
    j = pl.program_id(0)

    @pl.when(j < nu_ref[0])
    def _():
        base = j * MOE_RB

        def gather_row(r, carry):
            t = tok_ref[base + r]
            pltpu.make_async_copy(h_hbm.at[pl.ds(t, 1), :], xbuf.at[pl.ds(r, 1), :], gsem).start()
            return carry

        lax.fori_loop(0, MOE_RB, gather_row, 0)
        pltpu.make_async_copy(h_hbm.at[pl.ds(0, MOE_RB), :], xbuf, gsem).wait()

        x = xbuf[...].astype(BF16)
        gu = _dot(x, wgu_ref[...])
        gate = gu[:, :D_EXPERT]
        act = gate * jax.nn.sigmoid(gate) * gu[:, D_EXPERT:]
        ybuf[...] = _dot(act.astype(BF16), wd_ref[...])

        def scatter_row(r, carry):
            d = dst_ref[base + r]
            pltpu.make_async_copy(ybuf.at[pl.ds(r, 1), :], ff_hbm.at[pl.ds(d, 1), :], ssem).start()
            return carry

        lax.fori_loop(0, MOE_RB, scatter_row, 0)
        pltpu.make_async_copy(ybuf, ff_hbm.at[pl.ds(0, MOE_RB), :], ssem).wait()


def _moe(h, plan, w_gate_up_b, w_down_b, layer):
    block_e, n_used, row_tok, row_dst = plan
    return pl.pallas_call(
        _moe_body,
        grid_spec=pltpu.PrefetchScalarGridSpec(
            num_scalar_prefetch=4,
            grid=(MOE_NB,),
            in_specs=[
                pl.BlockSpec(memory_space=pl.ANY),
                pl.BlockSpec((None, None, D_MODEL, 2 * D_EXPERT), lambda j, be, nu, tk, ds: (layer, be[j], 0, 0)),
                pl.BlockSpec((None, None, D_EXPERT, D_MODEL), lambda j, be, nu, tk, ds: (layer, be[j], 0, 0)),
            ],
            out_specs=pl.BlockSpec(memory_space=pl.ANY),
            scratch_shapes=[
                pltpu.VMEM((MOE_RB, D_MODEL), F32),
                pltpu.VMEM((MOE_RB, D_MODEL), F32),
                pltpu.SemaphoreType.DMA(()),
                pltpu.SemaphoreType.DMA(()),
            ],
        ),
        out_shape=jax.ShapeDtypeStruct((MOE_NB * MOE_RB, D_MODEL), F32),
        compiler_params=_cparams(("arbitrary",), 48),
        name="moe_experts",
    )(block_e, n_used, row_tok, row_dst, h, w_gate_up_b, w_down_b)


def _post_body(h_ref, y0_ref, y1_ref, rw_ref, p_ref, wg_ref, wp_ref, g_ref, b_ref, bg_ref, x_ref, xb_ref):
    rw = rw_ref[...]
    ff = y0_ref[...] * rw[:, 0:1] + y1_ref[...] * rw[:, 1:2]
    h = _layer_norm(ALPHA * h_ref[...] + ff, g_ref[...], b_ref[...])
    gate = jax.nn.sigmoid(_dot(h.astype(BF16), wg_ref[...]) + bg_ref[...])
    pe = _dot(p_ref[...].astype(BF16), wp_ref[...].astype(BF16))
    x = h + gate * pe
    x_ref[...] = x
    xb_ref[...] = x.astype(BF16)


def _post(h, ff, rw, p, w_ple_gate_b, w_ple_proj, ln_g, ln_b, b_gate, layer):
    rows = lambda w: pl.BlockSpec((ROW_TM, w), lambda i: (i, 0))
    vec = lambda: pl.BlockSpec((None, 1, D_MODEL), lambda i: (layer, 0, 0))
    slot1 = N_TOK // ROW_TM
    return pl.pallas_call(
        _post_body,
        grid=(N_TOK // ROW_TM,),
        in_specs=[rows(D_MODEL),
                  pl.BlockSpec((ROW_TM, D_MODEL), lambda i: (i, 0)),
                  pl.BlockSpec((ROW_TM, D_MODEL), lambda i: (i + slot1, 0)),
                  rows(LANES),
                  pl.BlockSpec((None, ROW_TM, PLE_DIM), lambda i: (layer, i, 0)),
                  pl.BlockSpec((None, D_MODEL, D_MODEL), lambda i: (layer, 0, 0)),
                  pl.BlockSpec((None, PLE_DIM, D_MODEL), lambda i: (layer, 0, 0)),
                  vec(), vec(), vec()],
        out_specs=[rows(D_MODEL), rows(D_MODEL)],
        out_shape=[jax.ShapeDtypeStruct((N_TOK, D_MODEL), F32),
                   jax.ShapeDtypeStruct((N_TOK, D_MODEL), BF16)],
        compiler_params=_cparams(("parallel",), 56),
        name="combine_ln_ple",
    )(h, ff, ff, rw, p, w_ple_gate_b, w_ple_proj, ln_g, ln_b, b_gate)


def kernel(x_prompt, x_sample, cache_win_k, cache_win_v, state_ret, p_prompt, p_sample,
           w_in, w_out, ln1_g, ln1_b, ln2_g, ln2_b, w_router, b_router,
           w_gate_up, w_down, w_ple_proj, w_ple_gate, b_ple_gate):
    x = jnp.concatenate([x_prompt.reshape(N_PROMPT, D_MODEL), x_sample.reshape(DEC_BATCH, D_MODEL)], axis=0)
    xb = x.astype(BF16)
    p_all = jnp.concatenate([p_prompt.reshape(DEPTH, N_PROMPT, PLE_DIM),
                             p_sample.reshape(DEPTH, DEC_BATCH, PLE_DIM)], axis=1)
    w_out_b = w_out.astype(BF16)
    w_gate_up_b = w_gate_up.astype(BF16)
    w_down_b = w_down.astype(BF16)
    w_ple_gate_b = w_ple_gate.astype(BF16)
    w_router_p = jnp.pad(w_router, ((0, 0), (0, LANES - N_EXPERTS)))
    b_router_p = jnp.pad(b_router, (0, LANES - N_EXPERTS)).reshape(1, LANES)
    vec3 = lambda t: t.reshape(DEPTH, 1, D_MODEL)
    ln1_g, ln1_b, ln2_g, ln2_b, b_ple_gate = map(vec3, (ln1_g, ln1_b, ln2_g, ln2_b, b_ple_gate))

    tables = _retention_tables()
    gammas = tuple(float(1.0 - 2.0 ** (-5.0 - hh)) for hh in range(R_HEADS))
    rope_p = _rope_tables(jnp.arange(SEQ, dtype=F32))
    rope_s = _rope_tables(PAST_LEN + jnp.arange(1, dtype=F32))
    eye = jnp.eye(R_QK_DIM, dtype=BF16)

    win_k = win_v = st_s = None
    pk, pv, ps = [], [], []
    for layer in range(DEPTH):
        h = _inproj(xb, w_in, layer)
        attn_p = _attn_prompt(h)
        ret_p, st_p = _ret_prompt(h, tables, rope_p)
        hs = h[N_PROMPT:]
        heads = lambda off: hs[:, off:off + A_WIDTH].reshape(DEC_BATCH, A_HEADS, A_HEAD_DIM)
        attn_s, win_k, win_v = _sample_attn(heads(OFF_QA), heads(OFF_KA), heads(OFF_VA),
                                            cache_win_k, cache_win_v, layer, win_k, win_v)
        wide = lambda off, w: hs[:, off:off + w].reshape(DEC_BATCH, 1, w)
        ret_s, st_s = _sample_ret(wide(OFF_QR, R_HEADS * R_QK_DIM), wide(OFF_KR, R_HEADS * R_QK_DIM),
                                  wide(OFF_VR, R_WIDTH), wide(OFF_GR, R_WIDTH),
                                  state_ret, layer, rope_s, eye, gammas, st_s)
        attn = jnp.concatenate([attn_p, attn_s.reshape(DEC_BATCH, A_WIDTH).astype(BF16)], axis=0)
        ret = jnp.concatenate([ret_p, ret_s.reshape(DEC_BATCH, R_WIDTH).astype(BF16)], axis=0)
        h1, ridx, rw = _outproj(attn, ret, x, w_out_b, ln1_g, ln1_b, w_router_p, b_router_p, layer)
        ff = _moe(h1, _dispatch_plan(ridx), w_gate_up_b, w_down_b, layer)
        x, xb = _post(h1, ff, rw, p_all, w_ple_gate_b, w_ple_proj, ln2_g, ln2_b, b_ple_gate, layer)
        pk.append(h[:N_PROMPT, OFF_KA:OFF_KA + A_WIDTH].reshape(BATCH, SEQ, A_HEADS, A_HEAD_DIM))
        pv.append(h[:N_PROMPT, OFF_VA:OFF_VA + A_WIDTH].reshape(BATCH, SEQ, A_HEADS, A_HEAD_DIM))
        ps.append(st_p)

    y_prompt = x[:N_PROMPT].reshape(BATCH, SEQ, D_MODEL)
    y_sample = x[N_PROMPT:].reshape(DEC_BATCH, 1, D_MODEL)
    return (y_prompt, y_sample, jnp.stack(pk), jnp.stack(pv), jnp.stack(ps), win_k, win_v, st_s)
```

```python
import functools

import jax
import jax.numpy as jnp
from jax import lax
from jax.experimental import pallas as pl
from jax.experimental.pallas import tpu as pltpu

F32 = jnp.float32
BF16 = jnp.bfloat16

D_MODEL = 2048
BATCH = 4
SEQ = 2048
DEPTH = 2
DEC_BATCH = 128
PAST_LEN = 2048
N_PROMPT = BATCH * SEQ
N_TOK = N_PROMPT + DEC_BATCH

A_HEADS = 8
A_HEAD_DIM = 128
A_WIDTH = A_HEADS * A_HEAD_DIM
DILATED_PATTERNS = ((128, 1), (512, 4), (2048, 16))
W_MAX = 2048
QB = 128
R_HEADS = 4
R_QK_DIM = 256
R_V_DIM = 256
R_WIDTH = R_HEADS * R_V_DIM
R_CHUNK = 128
ROPE_BASE = 10000.0
OFF_QA = 0
OFF_KA = A_WIDTH
OFF_VA = 2 * A_WIDTH
OFF_QR = 3 * A_WIDTH
OFF_KR = OFF_QR + R_HEADS * R_QK_DIM
OFF_VR = OFF_KR + R_HEADS * R_QK_DIM
OFF_GR = OFF_VR + R_WIDTH
IN_COLS = OFF_GR + R_WIDTH
N_EXPERTS = 16
N_GROUPS = 4
EXPERTS_PER_GROUP = N_EXPERTS // N_GROUPS
TOP_K = 2
D_EXPERT = 1024
PLE_DIM = 256
ALPHA = (2 * DEPTH) ** 0.25
LN_EPS = 1e-5
NEG = -1e30

LANES = 128
MIB = 1024 * 1024

INPROJ_TM = 1664
INPROJ_TN = 512
ROW_TM = 320
MOE_RB = 256
N_ASSIGN = N_TOK * TOP_K
MOE_NB = (N_ASSIGN + N_EXPERTS * (MOE_RB - 1) + MOE_RB - 1) // MOE_RB
MOE_CAP = (N_TOK + MOE_RB - 1) // MOE_RB * MOE_RB
MOE_CAP_BLOCKS = MOE_CAP // MOE_RB
SAMPLE_CH = 512
ATTN_UNROLL = 4
DMA_UNROLL = 8


def _cparams(sem, vmem_mib):
    return pltpu.CompilerParams(dimension_semantics=sem, vmem_limit_bytes=vmem_mib * MIB)


def _dot(a, b):
    return jnp.dot(a, b, preferred_element_type=F32)


def _dot_nt(a, b):
    return lax.dot_general(a, b, (((1,), (1,)), ((), ())), preferred_element_type=F32)


def _dot_tn(a, b):
    return lax.dot_general(a, b, (((0,), (0,)), ((), ())), preferred_element_type=F32)


def _inproj_body(x_ref, w_ref, o_ref):
    o_ref[...] = _dot(x_ref[...], w_ref[...].astype(BF16))


def _inproj(xb, w_in, layer):
    return pl.pallas_call(
        _inproj_body,
        grid=(N_TOK // INPROJ_TM, IN_COLS // INPROJ_TN),
        in_specs=[
            pl.BlockSpec((INPROJ_TM, D_MODEL), lambda i, j: (i, 0)),
            pl.BlockSpec((None, D_MODEL, INPROJ_TN), lambda i, j: (layer, 0, j)),
        ],
        out_specs=pl.BlockSpec((INPROJ_TM, INPROJ_TN), lambda i, j: (i, j)),
        out_shape=jax.ShapeDtypeStruct((N_TOK, IN_COLS), F32),
        compiler_params=_cparams(("parallel", "arbitrary"), 48),
        name="inproj",
    )(xb, w_in)


def _attn_prompt_body(q_ref, k_ref, v_ref, o_ref, o_scr, lse_scr):
    scale = A_HEAD_DIM ** -0.5
    qi = lax.broadcasted_iota(jnp.int32, (QB, QB), 0)
    ki = lax.broadcasted_iota(jnp.int32, (QB, QB), 1)
    cur_ok = qi >= ki
    prev_ok = ki >= qi

    for p, (window, dil) in enumerate(DILATED_PATTERNS):
        assert window // dil == QB
        nb = (SEQ // dil) // QB

        def body(it, carry, p=p, dil=dil, nb=nb):
            r = it // nb
            n = it % nb
            start = r + n * (QB * dil)
            rows = pl.ds(start, QB, stride=dil) if dil > 1 else pl.ds(pl.multiple_of(start, QB), QB)
            q = q_ref[rows, :].astype(BF16)
            k = k_ref[rows, :].astype(BF16)
            v = v_ref[rows, :].astype(BF16)
            ok = cur_ok
            if nb > 1:
                pstart = r + jnp.maximum(n - 1, 0) * (QB * dil)
                prow = (pl.ds(pstart, QB, stride=dil) if dil > 1
                        else pl.ds(pl.multiple_of(pstart, QB), QB))
                k = jnp.concatenate([k_ref[prow, :].astype(BF16), k], axis=0)
                v = jnp.concatenate([v_ref[prow, :].astype(BF16), v], axis=0)
                ok = jnp.concatenate([prev_ok & (n > 0), cur_ok], axis=1)
            s = jnp.where(ok, _dot_nt(q, k) * scale, NEG)
            m = jnp.max(s, axis=-1, keepdims=True)
            pr = jnp.exp(s - m)
            l = jnp.sum(pr, axis=-1, keepdims=True)
            acc = _dot(pr.astype(BF16), v)
            o_scr[p, rows, :] = acc / l
            lse_scr[p, rows, :] = jnp.broadcast_to(m + jnp.log(l), (QB, LANES))
            return carry

        lax.fori_loop(0, dil * nb, body, 0, unroll=ATTN_UNROLL)

    def mix(c, carry):
        rows = pl.ds(pl.multiple_of(c * QB, QB), QB)
        l0 = lse_scr[0, rows, :]
        l1 = lse_scr[1, rows, :]
        l2 = lse_scr[2, rows, :]
        mx = jnp.maximum(jnp.maximum(l0, l1), l2)
        w0 = jnp.exp(l0 - mx)
        w1 = jnp.exp(l1 - mx)
        w2 = jnp.exp(l2 - mx)
        num = w0 * o_scr[0, rows, :] + w1 * o_scr[1, rows, :] + w2 * o_scr[2, rows, :]
        o_ref[rows, :] = (num / (w0 + w1 + w2)).astype(o_ref.dtype)
        return carry

    lax.fori_loop(0, SEQ // QB, mix, 0)


def _attn_prompt(h):
    col = lambda off: (lambda b, hh: (b, off // A_HEAD_DIM + hh))
    return pl.pallas_call(
        _attn_prompt_body,
        grid=(BATCH, A_HEADS),
        in_specs=[
            pl.BlockSpec((SEQ, A_HEAD_DIM), col(OFF_QA)),
            pl.BlockSpec((SEQ, A_HEAD_DIM), col(OFF_KA)),
            pl.BlockSpec((SEQ, A_HEAD_DIM), col(OFF_VA)),
        ],
        out_specs=pl.BlockSpec((SEQ, A_HEAD_DIM), lambda b, hh: (b, hh)),
        out_shape=jax.ShapeDtypeStruct((N_PROMPT, A_WIDTH), BF16),
        scratch_shapes=[
            pltpu.VMEM((len(DILATED_PATTERNS), SEQ, A_HEAD_DIM), F32),
            pltpu.VMEM((len(DILATED_PATTERNS), SEQ, LANES), F32),
        ],
        compiler_params=_cparams(("parallel", "parallel"), 32),
        name="attn_prompt",
    )(h, h, h)


def _rotate(x, cos, sin):
    half = x.shape[-1] // 2
    x1 = x[:, :half]
    x2 = x[:, half:]
    return jnp.concatenate([x1 * cos - x2 * sin, x1 * sin + x2 * cos], axis=-1)


def _group_norm_gate(o, g):
    mu = jnp.mean(o, axis=-1, keepdims=True)
    var = jnp.mean(jnp.square(o - mu), axis=-1, keepdims=True)
    on = (o - mu) * lax.rsqrt(var + LN_EPS)
    return on * (g * jax.nn.sigmoid(g))


def _ret_prompt_body(q_ref, k_ref, v_ref, g_ref, cos_ref, sin_ref, dmask_ref, qdec_ref, kdec_ref,
                     cdec_ref, o_ref, st_ref, st_scr):
    st_scr[...] = jnp.zeros_like(st_scr)
    dmask = dmask_ref[...]
    qdec = qdec_ref[...]
    kdec = kdec_ref[...]
    cdec = cdec_ref[...]

    def chunk(c, carry):
        rows = pl.ds(pl.multiple_of(c * R_CHUNK, R_CHUNK), R_CHUNK)
        cos = cos_ref[rows, :]
        sin = sin_ref[rows, :]
        q = _rotate(q_ref[rows, :], cos, sin)
        k = _rotate(k_ref[rows, :], cos, sin) * (R_QK_DIM ** -0.5)
        v = v_ref[rows, :].astype(BF16)
        st = st_scr[...]
        inner = _dot_nt(q.astype(BF16), k.astype(BF16)) * dmask
        o = _dot(inner.astype(BF16), v) + _dot((q * qdec).astype(BF16), st.astype(BF16))
        st_scr[...] = cdec * st + _dot_tn((k * kdec).astype(BF16), v)
        o_ref[rows, :] = _group_norm_gate(o, g_ref[rows, :]).astype(o_ref.dtype)
        return carry

    lax.fori_loop(0, SEQ // R_CHUNK, chunk, 0)
    st_ref[...] = st_scr[...]


def _retention_tables():
    lg = jnp.log1p(-jnp.exp2(-5.0 - jnp.arange(R_HEADS, dtype=F32)))
    i = jnp.arange(R_CHUNK, dtype=F32)
    rel = i[:, None] - i[None, :]
    dmask = jnp.exp(jnp.where(rel[None] >= 0, rel[None] * lg[:, None, None], -jnp.inf))
    qdec = jnp.exp((i[None, :] + 1.0) * lg[:, None])
    kdec = jnp.exp((R_CHUNK - 1.0 - i)[None, :] * lg[:, None])
    cdec = jnp.exp(R_CHUNK * lg)
    bcast = lambda t: jnp.broadcast_to(t[:, :, None], (R_HEADS, R_CHUNK, R_QK_DIM))
    return (dmask, bcast(qdec), bcast(kdec),
            jnp.broadcast_to(cdec[:, None, None], (R_HEADS, R_QK_DIM, R_V_DIM)), lg)


def _rope_tables(pos):
    half = R_QK_DIM // 2
    inv = 1.0 / (ROPE_BASE ** jnp.linspace(0.0, 1.0, half, dtype=F32))
    ang = pos[:, None] * inv[None, :]
    return jnp.cos(ang), jnp.sin(ang)


def _ret_prompt(h, tables, rope):
    dmask, qdec, kdec, cdec, _ = tables
    cos, sin = rope
    col = lambda off: (lambda b, hh: (b, off // R_QK_DIM + hh))
    per_head = lambda shape: pl.BlockSpec((None,) + shape, lambda b, hh: (hh, 0, 0))
    return pl.pallas_call(
        _ret_prompt_body,
        grid=(BATCH, R_HEADS),
        in_specs=[
            pl.BlockSpec((SEQ, R_QK_DIM), col(OFF_QR)),
            pl.BlockSpec((SEQ, R_QK_DIM), col(OFF_KR)),
            pl.BlockSpec((SEQ, R_V_DIM), col(OFF_VR)),
            pl.BlockSpec((SEQ, R_V_DIM), col(OFF_GR)),
            pl.BlockSpec((SEQ, R_QK_DIM // 2), lambda b, hh: (0, 0)),
            pl.BlockSpec((SEQ, R_QK_DIM // 2), lambda b, hh: (0, 0)),
            per_head((R_CHUNK, R_CHUNK)),
            per_head((R_CHUNK, R_QK_DIM)),
            per_head((R_CHUNK, R_QK_DIM)),
            per_head((R_QK_DIM, R_V_DIM)),
        ],
        out_specs=[
            pl.BlockSpec((SEQ, R_V_DIM), lambda b, hh: (b, hh)),
            pl.BlockSpec((None, None, R_QK_DIM, R_V_DIM), lambda b, hh: (b, hh, 0, 0)),
        ],
        out_shape=[
            jax.ShapeDtypeStruct((N_PROMPT, R_WIDTH), BF16),
            jax.ShapeDtypeStruct((BATCH, R_HEADS, R_QK_DIM, R_V_DIM), F32),
        ],
        scratch_shapes=[pltpu.VMEM((R_QK_DIM, R_V_DIM), F32)],
        compiler_params=_cparams(("parallel", "parallel"), 48),
        name="ret_prompt",
    )(h, h, h, h, cos, sin, dmask, qdec, kdec, cdec)


def _sample_attn_body(*refs, n_alias):
    q_ref, kn_ref, vn_ref, ck_ref, ckn_ref, cv_ref, cvn_ref = refs[:7]
    o_ref, ok_ref, ov_ref, m_scr, l_scr, acc_scr = refs[7 + n_alias:]
    c = pl.program_id(1)
    last = pl.num_programs(1) - 1
    scale = A_HEAD_DIM ** -0.5
    q = q_ref[...]

    @pl.when(c == 0)
    def _():
        m_scr[...] = jnp.full_like(m_scr, NEG)
        l_scr[...] = jnp.zeros_like(l_scr)
        acc_scr[...] = jnp.zeros_like(acc_scr)

    def accumulate(kr, vr, mult):
        s = jnp.sum(kr * q[None], axis=-1, keepdims=True) * scale
        m_old = m_scr[...]
        m_new = jnp.maximum(m_old, jnp.max(s, axis=0))
        alpha = jnp.exp(m_old - m_new)
        pr = jnp.exp(s - m_new[None])
        l_scr[...] = alpha * l_scr[...] + mult * jnp.sum(pr, axis=0)
        acc_scr[...] = alpha * acc_scr[...] + mult * jnp.sum(pr * vr, axis=0)
        m_scr[...] = m_new

    accumulate(ck_ref[pl.ds(0, SAMPLE_CH // 16, stride=16)], cv_ref[pl.ds(0, SAMPLE_CH // 16, stride=16)], 1.0)

    @pl.when(c == last)
    def _():
        accumulate(ck_ref[pl.ds(0, QB, stride=4)], cv_ref[pl.ds(0, QB, stride=4)], 1.0)
        accumulate(ck_ref[pl.ds(SAMPLE_CH - QB, QB)], cv_ref[pl.ds(SAMPLE_CH - QB, QB)], 1.0)
        accumulate(kn_ref[...][None], vn_ref[...][None], float(len(DILATED_PATTERNS)))
        o_ref[...] = acc_scr[...] / l_scr[...]

    ok_ref[pl.ds(0, SAMPLE_CH - 1)] = ck_ref[pl.ds(1, SAMPLE_CH - 1)]
    ov_ref[pl.ds(0, SAMPLE_CH - 1)] = cv_ref[pl.ds(1, SAMPLE_CH - 1)]
    is_last = c == last
    ok_ref[SAMPLE_CH - 1] = jnp.where(is_last, kn_ref[...], ckn_ref[0])
    ov_ref[SAMPLE_CH - 1] = jnp.where(is_last, vn_ref[...], cvn_ref[0])


def _sample_attn(q_s, k_s, v_s, cache_k, cache_v, layer, prev_k=None, prev_v=None):
    assert PAST_LEN == W_MAX and W_MAX % SAMPLE_CH == 0 and SAMPLE_CH == 512
    nc = W_MAX // SAMPLE_CH
    row = pl.BlockSpec((None, A_HEADS, A_HEAD_DIM), lambda b, c: (b, 0, 0))
    chunk = pl.BlockSpec((None, None, SAMPLE_CH, A_HEADS, A_HEAD_DIM), lambda b, c: (layer, b, c, 0, 0))
    nxt = pl.BlockSpec((None, None, 1, A_HEADS, A_HEAD_DIM),
                       lambda b, c: (layer, b, jnp.minimum((c + 1) * SAMPLE_CH, W_MAX - 1), 0, 0))
    n_alias = 0 if prev_k is None else 2
    alias_specs = [pl.BlockSpec(memory_space=pl.ANY)] * n_alias
    alias_args = [] if prev_k is None else [prev_k, prev_v]
    win_shape = jax.ShapeDtypeStruct((DEPTH, DEC_BATCH, W_MAX, A_HEADS, A_HEAD_DIM), F32)
    return pl.pallas_call(
        functools.partial(_sample_attn_body, n_alias=n_alias),
        grid=(DEC_BATCH, nc),
        in_specs=[row, row, row, chunk, nxt, chunk, nxt] + alias_specs,
        out_specs=[row, chunk, chunk],
        out_shape=[jax.ShapeDtypeStruct((DEC_BATCH, A_HEADS, A_HEAD_DIM), F32), win_shape, win_shape],
        scratch_shapes=[pltpu.VMEM((A_HEADS, A_HEAD_DIM), F32)] * 3,
        input_output_aliases={7: 1, 8: 2} if n_alias else {},
        compiler_params=_cparams(("parallel", "arbitrary"), 40),
        name="sample_attn",
    )(q_s, k_s, v_s, cache_k, cache_k, cache_v, cache_v, *alias_args)


def _sample_ret_body(*refs, n_alias, gammas):
    q_ref, k_ref, v_ref, g_ref, cos_ref, sin_ref, eye_ref, st_ref = refs[:8]
    o_ref, sto_ref = refs[8 + n_alias:]
    cos = cos_ref[...]
    sin = sin_ref[...]
    eye = eye_ref[...]
    outs = []
    for hh in range(R_HEADS):
        cols = slice(hh * R_QK_DIM, (hh + 1) * R_QK_DIM)
        q = _rotate(q_ref[:, cols], cos, sin)
        k = _rotate(k_ref[:, cols], cos, sin) * (R_QK_DIM ** -0.5)
        v = v_ref[:, cols]
        st = st_ref[hh]
        inner = jnp.sum(q * k, axis=-1, keepdims=True)
        q_dec = jnp.broadcast_to(q * gammas[hh], (8, R_QK_DIM)).astype(BF16)
        o = inner * v + _dot(q_dec, st.astype(BF16))[0:1]
        k_col = _dot_nt(eye, jnp.broadcast_to(k, (R_QK_DIM, R_QK_DIM)).astype(BF16))
        sto_ref[hh] = gammas[hh] * st + k_col * v
        outs.append(_group_norm_gate(o, g_ref[:, cols]))
    o_ref[...] = jnp.concatenate(outs, axis=-1)


def _sample_ret(q_s, k_s, v_s, g_s, state, layer, rope_s, eye, gammas, prev=None):
    row = lambda w: pl.BlockSpec((None, 1, w), lambda b: (b, 0, 0))
    const = lambda shape: pl.BlockSpec(shape, lambda b: (0,) * len(shape))
    st_spec = pl.BlockSpec((None, None, R_HEADS, R_QK_DIM, R_V_DIM), lambda b: (layer, b, 0, 0, 0))
    n_alias = 0 if prev is None else 1
    return pl.pallas_call(
        functools.partial(_sample_ret_body, n_alias=n_alias, gammas=gammas),
        grid=(DEC_BATCH,),
        in_specs=[row(R_HEADS * R_QK_DIM), row(R_HEADS * R_QK_DIM), row(R_WIDTH), row(R_WIDTH),
                  const((1, R_QK_DIM // 2)), const((1, R_QK_DIM // 2)), const((R_QK_DIM, R_QK_DIM)), st_spec]
        + [pl.BlockSpec(memory_space=pl.ANY)] * n_alias,
        out_specs=[row(R_WIDTH), st_spec],
        out_shape=[jax.ShapeDtypeStruct((DEC_BATCH, 1, R_WIDTH), F32),
                   jax.ShapeDtypeStruct((DEPTH, DEC_BATCH, R_HEADS, R_QK_DIM, R_V_DIM), F32)],
        input_output_aliases={8: 1} if n_alias else {},
        compiler_params=_cparams(("parallel",), 32),
        name="sample_ret",
    )(q_s, k_s, v_s, g_s, rope_s[0], rope_s[1], eye, state, *([] if prev is None else [prev]))


def _layer_norm(y, g, b):
    mu = jnp.mean(y, axis=-1, keepdims=True)
    var = jnp.mean(jnp.square(y - mu), axis=-1, keepdims=True)
    return (y - mu) * lax.rsqrt(var + LN_EPS) * g + b


def _split_bf16(x):
    hi = x.astype(BF16)
    lo = (x - hi.astype(F32)).astype(BF16)
    return hi, lo


def _outproj_body(attn_ref, ret_ref, x_ref, w_ref, g_ref, b_ref, wr_ref, br_ref,
                  h_ref, rdst_ref, rw_ref, cnt_ref, run_scr):
    @pl.when(pl.program_id(0) == 0)
    def _():
        run_scr[...] = jnp.zeros_like(run_scr)

    acc = _dot(attn_ref[...], w_ref[0:A_WIDTH, :]) + _dot(ret_ref[...], w_ref[A_WIDTH:, :])
    h = _layer_norm(ALPHA * x_ref[...] + acc, g_ref[...], b_ref[...])
    h_ref[...] = h

    h_hi, h_lo = _split_bf16(h)
    w_hi, w_lo = _split_bf16(wr_ref[...])
    logits = _dot(h_hi, w_hi) + _dot(h_hi, w_lo) + _dot(h_lo, w_hi) + br_ref[...]
    lane = lax.broadcasted_iota(jnp.int32, logits.shape, 1)
    valid = lane < N_EXPERTS
    logits = jnp.where(valid, logits, NEG)
    e = jnp.exp(logits - jnp.max(logits, axis=-1, keepdims=True))
    probs = e / jnp.sum(e, axis=-1, keepdims=True)
    grp = lane // EXPERTS_PER_GROUP
    best = jnp.max(jnp.where(grp == 0, probs, -1.0), axis=-1, keepdims=True)
    sel = jnp.zeros_like(best, dtype=jnp.int32)
    for gi in range(1, N_GROUPS):
        gm = jnp.max(jnp.where(grp == gi, probs, -1.0), axis=-1, keepdims=True)
        better = gm > best
        sel = jnp.where(better, gi, sel)
        best = jnp.maximum(best, gm)
    pv = jnp.where(grp == sel, probs, -1.0)
    v1 = jnp.max(pv, axis=-1, keepdims=True)
    i1 = jnp.min(jnp.where(pv == v1, lane, LANES), axis=-1, keepdims=True)
    pv2 = jnp.where(lane == i1, -1.0, pv)
    v2 = jnp.max(pv2, axis=-1, keepdims=True)
    i2 = jnp.min(jnp.where(pv2 == v2, lane, LANES), axis=-1, keepdims=True)
    tot = v1 + v2
    rw_ref[...] = jnp.where(lane == 0, v1 / tot, jnp.where(lane == 1, v2 / tot, 0.0))

    pick1 = lane == i1
    pick2 = lane == i2
    member = jnp.where(pick1 | pick2, 1.0, 0.0)
    ri = lax.broadcasted_iota(jnp.int32, (ROW_TM, ROW_TM), 0)
    ci = lax.broadcasted_iota(jnp.int32, (ROW_TM, ROW_TM), 1)
    earlier = jnp.where(ci < ri, 1.0, 0.0).astype(BF16)
    run = run_scr[...]
    rank = _dot(earlier, member.astype(BF16)) + run
    rank1 = jnp.sum(jnp.where(pick1, rank, 0.0), axis=-1, keepdims=True).astype(jnp.int32)
    rank2 = jnp.sum(jnp.where(pick2, rank, 0.0), axis=-1, keepdims=True).astype(jnp.int32)
    rdst_ref[...] = jnp.where(lane == 0, i1 * MOE_CAP + rank1, jnp.where(lane == 1, i2 * MOE_CAP + rank2, 0))
    run = run + jnp.sum(member, axis=0, keepdims=True)
    run_scr[...] = run
    cnt_ref[...] = jnp.broadcast_to(run, cnt_ref.shape).astype(jnp.int32)


def _outproj(attn, ret, x, w_out_b, ln_g, ln_b, w_router_p, b_router_p, layer):
    rows = lambda w: pl.BlockSpec((ROW_TM, w), lambda i: (i, 0))
    vec = lambda: pl.BlockSpec((None, 1, D_MODEL), lambda i: (layer, 0, 0))
    return pl.pallas_call(
        _outproj_body,
        grid=(N_TOK // ROW_TM,),
        in_specs=[rows(A_WIDTH), rows(R_WIDTH), rows(D_MODEL),
                  pl.BlockSpec((None, A_WIDTH + R_WIDTH, D_MODEL), lambda i: (layer, 0, 0)),
                  vec(), vec(),
                  pl.BlockSpec((D_MODEL, LANES), lambda i: (0, 0)),
                  pl.BlockSpec((1, LANES), lambda i: (0, 0))],
        out_specs=[rows(D_MODEL), rows(LANES), rows(LANES), pl.BlockSpec((8, LANES), lambda i: (0, 0))],
        out_shape=[jax.ShapeDtypeStruct((N_TOK, D_MODEL), F32),
                   jax.ShapeDtypeStruct((N_TOK, LANES), jnp.int32),
                   jax.ShapeDtypeStruct((N_TOK, LANES), F32),
                   jax.ShapeDtypeStruct((8, LANES), jnp.int32)],
        scratch_shapes=[pltpu.VMEM((1, LANES), F32)],
        compiler_params=_cparams(("arbitrary",), 48),
        name="outproj_ln_router",
    )(attn, ret, x, w_out_b, ln_g, ln_b, w_router_p, b_router_p)


def _block_map(counts):
    nblk = (counts + MOE_RB - 1) // MOE_RB
    cum = jnp.cumsum(nblk)
    n_used = cum[-1]
    j = jnp.minimum(jnp.arange(MOE_NB, dtype=jnp.int32), n_used - 1)
    e = jnp.minimum(jnp.sum(j[:, None] >= cum[None, :], axis=1), N_EXPERTS - 1).astype(jnp.int32)
    first = (cum - nblk)[e]
    blk_row = e * MOE_CAP_BLOCKS + (j - first)
    return blk_row.astype(jnp.int32), e, n_used.astype(jnp.int32).reshape(1)


def _row_copy(src, src_row, dst, dst_row, sem):
    return pltpu.make_async_copy(src.at[pl.ds(src_row, 1), :], dst.at[pl.ds(dst_row, 1), :], sem)


def _dispatch_body(dst_ref, cnt_ref, h_hbm, xs_hbm, zbuf, zsem, sem):
    i = pl.program_id(0)

    @pl.when(i == 0)
    def _():
        zbuf[...] = jnp.zeros_like(zbuf)

        def tail_copy(e):
            nblk = (cnt_ref[e] + MOE_RB - 1) // MOE_RB
            row0 = pl.multiple_of((e * MOE_CAP_BLOCKS + nblk - 1) * MOE_RB, MOE_RB)
            return pltpu.make_async_copy(zbuf, xs_hbm.at[pl.ds(row0, MOE_RB), :], zsem)

        for e in range(N_EXPERTS):
            @pl.when(cnt_ref[e] > 0)
            def _(e=e):
                tail_copy(e).start()
        for e in range(N_EXPERTS):
            @pl.when(cnt_ref[e] > 0)
            def _(e=e):
                tail_copy(e).wait()

    base = i * ROW_TM

    def scatter_row(r, carry):
        t = base + r
        _row_copy(h_hbm, t, xs_hbm, dst_ref[TOP_K * t], sem).start()
        _row_copy(h_hbm, t, xs_hbm, dst_ref[TOP_K * t + 1], sem).start()
        return carry

    lax.fori_loop(0, ROW_TM, scatter_row, 0, unroll=DMA_UNROLL)
    pltpu.make_async_copy(h_hbm.at[pl.ds(0, TOP_K * ROW_TM), :], xs_hbm.at[pl.ds(0, TOP_K * ROW_TM), :], sem).wait()


def _dispatch(h, dst, counts):
    return pl.pallas_call(
        _dispatch_body,
        grid_spec=pltpu.PrefetchScalarGridSpec(
            num_scalar_prefetch=2,
            grid=(N_TOK // ROW_TM,),
            in_specs=[pl.BlockSpec(memory_space=pl.ANY)],
            out_specs=pl.BlockSpec(memory_space=pl.ANY),
            scratch_shapes=[pltpu.VMEM((MOE_RB, D_MODEL), F32),
                            pltpu.SemaphoreType.DMA(()), pltpu.SemaphoreType.DMA(())],
        ),
        out_shape=jax.ShapeDtypeStruct((N_EXPERTS * MOE_CAP, D_MODEL), F32),
        compiler_params=_cparams(("arbitrary",), 16),
        name="moe_dispatch",
    )(dst, counts, h)


def _experts_body(br_ref, be_ref, nu_ref, x_ref, wgu_ref, wd_ref, y_ref):
    @pl.when(pl.program_id(0) < nu_ref[0])
    def _():
        gu = _dot(x_ref[...].astype(BF16), wgu_ref[...])
        gate = gu[:, :D_EXPERT]
        act = gate * jax.nn.sigmoid(gate) * gu[:, D_EXPERT:]
        y_ref[...] = _dot(act.astype(BF16), wd_ref[...])


def _experts(xs, block_map, w_gate_up_b, w_down_b, layer):
    blk_row, block_e, n_used = block_map
    rows = pl.BlockSpec((MOE_RB, D_MODEL), lambda j, br, be, nu: (br[j], 0))
    return pl.pallas_call(
        _experts_body,
        grid_spec=pltpu.PrefetchScalarGridSpec(
            num_scalar_prefetch=3,
            grid=(MOE_NB,),
            in_specs=[
                rows,
                pl.BlockSpec((None, None, D_MODEL, 2 * D_EXPERT), lambda j, br, be, nu: (layer, be[j], 0, 0)),
                pl.BlockSpec((None, None, D_EXPERT, D_MODEL), lambda j, br, be, nu: (layer, be[j], 0, 0)),
            ],
            out_specs=rows,
        ),
        out_shape=jax.ShapeDtypeStruct((N_EXPERTS * MOE_CAP, D_MODEL), F32),
        compiler_params=_cparams(("arbitrary",), 48),
        name="moe_experts",
    )(blk_row, block_e, n_used, xs, w_gate_up_b, w_down_b)


def _post_body(dst_ref, h_ref, rw_ref, p_ref, wg_ref, wp_ref, g_ref, b_ref, bg_ref, ys_hbm,
               x_ref, xb_ref, ybuf, sem):
    base = pl.program_id(0) * ROW_TM

    def gather_row(r, carry):
        a = TOP_K * (base + r)
        _row_copy(ys_hbm, dst_ref[a], ybuf, r, sem).start()
        _row_copy(ys_hbm, dst_ref[a + 1], ybuf, ROW_TM + r, sem).start()
        return carry

    lax.fori_loop(0, ROW_TM, gather_row, 0, unroll=DMA_UNROLL)
    pe = _dot(p_ref[...].astype(BF16), wp_ref[...].astype(BF16))
    pltpu.make_async_copy(ys_hbm.at[pl.ds(0, TOP_K * ROW_TM), :], ybuf, sem).wait()

    rw = rw_ref[...]
    ff = ybuf[0:ROW_TM, :] * rw[:, 0:1] + ybuf[ROW_TM:, :] * rw[:, 1:2]
    h = _layer_norm(ALPHA * h_ref[...] + ff, g_ref[...], b_ref[...])
    gate = jax.nn.sigmoid(_dot(h.astype(BF16), wg_ref[...]) + bg_ref[...])
    x = h + gate * pe
    x_ref[...] = x
    xb_ref[...] = x.astype(BF16)


def _post(h, ys, dst, rw, p, w_ple_gate_b, w_ple_proj, ln_g, ln_b, b_gate, layer):
    rows = lambda w: pl.BlockSpec((ROW_TM, w), lambda i, ds: (i, 0))
    vec = lambda: pl.BlockSpec((None, 1, D_MODEL), lambda i, ds: (layer, 0, 0))
    return pl.pallas_call(
        _post_body,
        grid_spec=pltpu.PrefetchScalarGridSpec(
            num_scalar_prefetch=1,
            grid=(N_TOK // ROW_TM,),
            in_specs=[rows(D_MODEL), rows(LANES),
                      pl.BlockSpec((None, ROW_TM, PLE_DIM), lambda i, ds: (layer, i, 0)),
                      pl.BlockSpec((None, D_MODEL, D_MODEL), lambda i, ds: (layer, 0, 0)),
                      pl.BlockSpec((None, PLE_DIM, D_MODEL), lambda i, ds: (layer, 0, 0)),
                      vec(), vec(), vec(),
                      pl.BlockSpec(memory_space=pl.ANY)],
            out_specs=[rows(D_MODEL), rows(D_MODEL)],
            scratch_shapes=[pltpu.VMEM((TOP_K * ROW_TM, D_MODEL), F32), pltpu.SemaphoreType.DMA(())],
        ),
        out_shape=[jax.ShapeDtypeStruct((N_TOK, D_MODEL), F32),
                   jax.ShapeDtypeStruct((N_TOK, D_MODEL), BF16)],
        compiler_params=_cparams(("arbitrary",), 56),
        name="combine_ln_ple",
    )(dst, h, rw, p, w_ple_gate_b, w_ple_proj, ln_g, ln_b, b_gate, ys)


def kernel(x_prompt, x_sample, cache_win_k, cache_win_v, state_ret, p_prompt, p_sample,
           w_in, w_out, ln1_g, ln1_b, ln2_g, ln2_b, w_router, b_router,
           w_gate_up, w_down, w_ple_proj, w_ple_gate, b_ple_gate):
    x = jnp.concatenate([x_prompt.reshape(N_PROMPT, D_MODEL), x_sample.reshape(DEC_BATCH, D_MODEL)], axis=0)
    xb = x.astype(BF16)
    p_all = jnp.concatenate([p_prompt.reshape(DEPTH, N_PROMPT, PLE_DIM),
                             p_sample.reshape(DEPTH, DEC_BATCH, PLE_DIM)], axis=1)
    w_out_b = w_out.astype(BF16)
    w_gate_up_b = w_gate_up.astype(BF16)
    w_down_b = w_down.astype(BF16)
    w_ple_gate_b = w_ple_gate.astype(BF16)
    w_router_p = jnp.pad(w_router, ((0, 0), (0, LANES - N_EXPERTS)))
    b_router_p = jnp.pad(b_router, (0, LANES - N_EXPERTS)).reshape(1, LANES)
    vec3 = lambda t: t.reshape(DEPTH, 1, D_MODEL)
    ln1_g, ln1_b, ln2_g, ln2_b, b_ple_gate = map(vec3, (ln1_g, ln1_b, ln2_g, ln2_b, b_ple_gate))

    tables = _retention_tables()
    gammas = tuple(float(1.0 - 2.0 ** (-5.0 - hh)) for hh in range(R_HEADS))
    rope_p = _rope_tables(jnp.arange(SEQ, dtype=F32))
    rope_s = _rope_tables(PAST_LEN + jnp.arange(1, dtype=F32))
    eye = jnp.eye(R_QK_DIM, dtype=BF16)

    win_k = win_v = st_s = None
    pk, pv, ps = [], [], []
    for layer in range(DEPTH):
        h = _inproj(xb, w_in, layer)
        attn_p = _attn_prompt(h)
        ret_p, st_p = _ret_prompt(h, tables, rope_p)
        hs = h[N_PROMPT:]
        heads = lambda off: hs[:, off:off + A_WIDTH].reshape(DEC_BATCH, A_HEADS, A_HEAD_DIM)
        attn_s, win_k, win_v = _sample_attn(heads(OFF_QA), heads(OFF_KA), heads(OFF_VA),
                                            cache_win_k, cache_win_v, layer, win_k, win_v)
        wide = lambda off, w: hs[:, off:off + w].reshape(DEC_BATCH, 1, w)
        ret_s, st_s = _sample_ret(wide(OFF_QR, R_HEADS * R_QK_DIM), wide(OFF_KR, R_HEADS * R_QK_DIM),
                                  wide(OFF_VR, R_WIDTH), wide(OFF_GR, R_WIDTH),
                                  state_ret, layer, rope_s, eye, gammas, st_s)
        attn = jnp.concatenate([attn_p, attn_s.reshape(DEC_BATCH, A_WIDTH).astype(BF16)], axis=0)
        ret = jnp.concatenate([ret_p, ret_s.reshape(DEC_BATCH, R_WIDTH).astype(BF16)], axis=0)
        h1, rdst, rw, cnt = _outproj(attn, ret, x, w_out_b, ln1_g, ln1_b, w_router_p, b_router_p, layer)
        dst = rdst[:, :TOP_K].reshape(N_ASSIGN)
        counts = cnt[0, :N_EXPERTS]
        xs = _dispatch(h1, dst, counts)
        ys = _experts(xs, _block_map(counts), w_gate_up_b, w_down_b, layer)
        x, xb = _post(h1, ys, dst, rw, p_all, w_ple_gate_b, w_ple_proj, ln2_g, ln2_b, b_ple_gate, layer)
        pk.append(h[:N_PROMPT, OFF_KA:OFF_KA + A_WIDTH].reshape(BATCH, SEQ, A_HEADS, A_HEAD_DIM))
        pv.append(h[:N_PROMPT, OFF_VA:OFF_VA + A_WIDTH].reshape(BATCH, SEQ, A_HEADS, A_HEAD_DIM))
        ps.append(st_p)

    y_prompt = x[:N_PROMPT].reshape(BATCH, SEQ, D_MODEL)
    y_sample = x[N_PROMPT:].reshape(DEC_BATCH, 1, D_MODEL)
    return (y_prompt, y_sample, jnp.stack(pk), jnp.stack(pv), jnp.stack(ps), win_k, win_v, st_s)
```

```python
import functools

import jax
import jax.numpy as jnp
from jax import lax
from jax.experimental import pallas as pl
from jax.experimental.pallas import tpu as pltpu

F32 = jnp.float32
BF16 = jnp.bfloat16

D_MODEL = 2048
BATCH = 4
SEQ = 2048
DEPTH = 2
DEC_BATCH = 128
PAST_LEN = 2048
N_PROMPT = BATCH * SEQ
N_TOK = N_PROMPT + DEC_BATCH

A_HEADS = 8
A_HEAD_DIM = 128
A_WIDTH = A_HEADS * A_HEAD_DIM
DILATED_PATTERNS = ((128, 1), (512, 4), (2048, 16))
W_MAX = 2048
QB = 128
R_HEADS = 4
R_QK_DIM = 256
R_V_DIM = 256
R_WIDTH = R_HEADS * R_V_DIM
R_CHUNK = 128
ROPE_BASE = 10000.0
OFF_QA = 0
OFF_KA = A_WIDTH
OFF_VA = 2 * A_WIDTH
OFF_QR = 3 * A_WIDTH
OFF_KR = OFF_QR + R_HEADS * R_QK_DIM
OFF_VR = OFF_KR + R_HEADS * R_QK_DIM
OFF_GR = OFF_VR + R_WIDTH
IN_COLS = OFF_GR + R_WIDTH
N_EXPERTS = 16
N_GROUPS = 4
EXPERTS_PER_GROUP = N_EXPERTS // N_GROUPS
TOP_K = 2
D_EXPERT = 1024
PLE_DIM = 256
ALPHA = (2 * DEPTH) ** 0.25
LN_EPS = 1e-5
NEG = -1e30

LANES = 128
MIB = 1024 * 1024

INPROJ_TM = 1664
INPROJ_TN = 512
ROW_TM = 320
MOE_RB = 256
N_ASSIGN = N_TOK * TOP_K
MOE_NB = (N_ASSIGN + N_EXPERTS * (MOE_RB - 1) + MOE_RB - 1) // MOE_RB
MOE_CAP = (N_TOK + MOE_RB - 1) // MOE_RB * MOE_RB
MOE_CAP_BLOCKS = MOE_CAP // MOE_RB
SAMPLE_CH = 512
ATTN_UNROLL = 8
DMA_UNROLL = 8


def _cparams(sem, vmem_mib):
    return pltpu.CompilerParams(dimension_semantics=sem, vmem_limit_bytes=vmem_mib * MIB)


def _dot(a, b):
    return jnp.dot(a, b, preferred_element_type=F32)


def _dot_nt(a, b):
    return lax.dot_general(a, b, (((1,), (1,)), ((), ())), preferred_element_type=F32)


def _dot_tn(a, b):
    return lax.dot_general(a, b, (((0,), (0,)), ((), ())), preferred_element_type=F32)


def _inproj_body(x_ref, w_ref, o_ref):
    o_ref[...] = _dot(x_ref[...], w_ref[...].astype(BF16))


def _inproj(xb, w_in, layer):
    return pl.pallas_call(
        _inproj_body,
        grid=(N_TOK // INPROJ_TM, IN_COLS // INPROJ_TN),
        in_specs=[
            pl.BlockSpec((INPROJ_TM, D_MODEL), lambda i, j: (i, 0)),
            pl.BlockSpec((None, D_MODEL, INPROJ_TN), lambda i, j: (layer, 0, j)),
        ],
        out_specs=pl.BlockSpec((INPROJ_TM, INPROJ_TN), lambda i, j: (i, j)),
        out_shape=jax.ShapeDtypeStruct((N_TOK, IN_COLS), F32),
        compiler_params=_cparams(("parallel", "arbitrary"), 48),
        name="inproj",
    )(xb, w_in)


def _attn_prompt_body(q_ref, k_ref, v_ref, o_ref, o_scr, lse_scr):
    scale = A_HEAD_DIM ** -0.5
    qi = lax.broadcasted_iota(jnp.int32, (QB, QB), 0)
    ki = lax.broadcasted_iota(jnp.int32, (QB, QB), 1)
    cur_ok = qi >= ki
    prev_ok = ki >= qi

    for p, (window, dil) in enumerate(DILATED_PATTERNS):
        assert window // dil == QB
        nb = (SEQ // dil) // QB

        def body(it, carry, p=p, dil=dil, nb=nb):
            r = it // nb
            n = it % nb
            start = r + n * (QB * dil)
            rows = pl.ds(start, QB, stride=dil) if dil > 1 else pl.ds(pl.multiple_of(start, QB), QB)
            q = q_ref[rows, :].astype(BF16)
            k = k_ref[rows, :].astype(BF16)
            v = v_ref[rows, :].astype(BF16)
            ok = cur_ok
            if nb > 1:
                pstart = r + jnp.maximum(n - 1, 0) * (QB * dil)
                prow = (pl.ds(pstart, QB, stride=dil) if dil > 1
                        else pl.ds(pl.multiple_of(pstart, QB), QB))
                k = jnp.concatenate([k_ref[prow, :].astype(BF16), k], axis=0)
                v = jnp.concatenate([v_ref[prow, :].astype(BF16), v], axis=0)
                ok = jnp.concatenate([prev_ok & (n > 0), cur_ok], axis=1)
            s = jnp.where(ok, _dot_nt(q, k) * scale, NEG)
            m = jnp.max(s, axis=-1, keepdims=True)
            pr = jnp.exp(s - m)
            l = jnp.sum(pr, axis=-1, keepdims=True)
            acc = _dot(pr.astype(BF16), v)
            o_scr[p, rows, :] = acc / l
            lse_scr[p, rows, :] = jnp.broadcast_to(m + jnp.log(l), (QB, LANES))
            return carry

        lax.fori_loop(0, dil * nb, body, 0, unroll=ATTN_UNROLL)

    def mix(c, carry):
        rows = pl.ds(pl.multiple_of(c * QB, QB), QB)
        l0 = lse_scr[0, rows, :]
        l1 = lse_scr[1, rows, :]
        l2 = lse_scr[2, rows, :]
        mx = jnp.maximum(jnp.maximum(l0, l1), l2)
        w0 = jnp.exp(l0 - mx)
        w1 = jnp.exp(l1 - mx)
        w2 = jnp.exp(l2 - mx)
        num = w0 * o_scr[0, rows, :] + w1 * o_scr[1, rows, :] + w2 * o_scr[2, rows, :]
        o_ref[rows, :] = (num / (w0 + w1 + w2)).astype(o_ref.dtype)
        return carry

    lax.fori_loop(0, SEQ // QB, mix, 0)


def _attn_prompt(h):
    col = lambda off: (lambda b, hh: (b, off // A_HEAD_DIM + hh))
    return pl.pallas_call(
        _attn_prompt_body,
        grid=(BATCH, A_HEADS),
        in_specs=[
            pl.BlockSpec((SEQ, A_HEAD_DIM), col(OFF_QA)),
            pl.BlockSpec((SEQ, A_HEAD_DIM), col(OFF_KA)),
            pl.BlockSpec((SEQ, A_HEAD_DIM), col(OFF_VA)),
        ],
        out_specs=pl.BlockSpec((SEQ, A_HEAD_DIM), lambda b, hh: (b, hh)),
        out_shape=jax.ShapeDtypeStruct((N_PROMPT, A_WIDTH), BF16),
        scratch_shapes=[
            pltpu.VMEM((len(DILATED_PATTERNS), SEQ, A_HEAD_DIM), F32),
            pltpu.VMEM((len(DILATED_PATTERNS), SEQ, LANES), F32),
        ],
        compiler_params=_cparams(("parallel", "parallel"), 32),
        name="attn_prompt",
    )(h, h, h)


def _rotate(x, cos, sin):
    half = x.shape[-1] // 2
    x1 = x[:, :half]
    x2 = x[:, half:]
    return jnp.concatenate([x1 * cos - x2 * sin, x1 * sin + x2 * cos], axis=-1)


def _group_norm_gate(o, g):
    mu = jnp.mean(o, axis=-1, keepdims=True)
    var = jnp.mean(jnp.square(o - mu), axis=-1, keepdims=True)
    on = (o - mu) * lax.rsqrt(var + LN_EPS)
    return on * (g * jax.nn.sigmoid(g))


def _ret_prompt_body(q_ref, k_ref, v_ref, g_ref, cos_ref, sin_ref, dmask_ref, qdec_ref, kdec_ref,
                     cdec_ref, o_ref, st_ref, st_scr):
    st_scr[...] = jnp.zeros_like(st_scr)
    dmask = dmask_ref[...]
    qdec = qdec_ref[...]
    kdec = kdec_ref[...]
    cdec = cdec_ref[...]

    def chunk(c, carry):
        rows = pl.ds(pl.multiple_of(c * R_CHUNK, R_CHUNK), R_CHUNK)
        cos = cos_ref[rows, :]
        sin = sin_ref[rows, :]
        q = _rotate(q_ref[rows, :], cos, sin)
        k = _rotate(k_ref[rows, :], cos, sin) * (R_QK_DIM ** -0.5)
        v = v_ref[rows, :].astype(BF16)
        st = st_scr[...]
        inner = _dot_nt(q.astype(BF16), k.astype(BF16)) * dmask
        o = _dot(inner.astype(BF16), v) + _dot((q * qdec).astype(BF16), st.astype(BF16))
        st_scr[...] = cdec * st + _dot_tn((k * kdec).astype(BF16), v)
        o_ref[rows, :] = _group_norm_gate(o, g_ref[rows, :]).astype(o_ref.dtype)
        return carry

    lax.fori_loop(0, SEQ // R_CHUNK, chunk, 0, unroll=2)
    st_ref[...] = st_scr[...]


def _retention_tables():
    lg = jnp.log1p(-jnp.exp2(-5.0 - jnp.arange(R_HEADS, dtype=F32)))
    i = jnp.arange(R_CHUNK, dtype=F32)
    rel = i[:, None] - i[None, :]
    dmask = jnp.exp(jnp.where(rel[None] >= 0, rel[None] * lg[:, None, None], -jnp.inf))
    qdec = jnp.exp((i[None, :] + 1.0) * lg[:, None])
    kdec = jnp.exp((R_CHUNK - 1.0 - i)[None, :] * lg[:, None])
    cdec = jnp.exp(R_CHUNK * lg)
    bcast = lambda t: jnp.broadcast_to(t[:, :, None], (R_HEADS, R_CHUNK, R_QK_DIM))
    return (dmask, bcast(qdec), bcast(kdec),
            jnp.broadcast_to(cdec[:, None, None], (R_HEADS, R_QK_DIM, R_V_DIM)), lg)


def _rope_tables(pos):
    half = R_QK_DIM // 2
    inv = 1.0 / (ROPE_BASE ** jnp.linspace(0.0, 1.0, half, dtype=F32))
    ang = pos[:, None] * inv[None, :]
    return jnp.cos(ang), jnp.sin(ang)


def _ret_prompt(h, tables, rope):
    dmask, qdec, kdec, cdec, _ = tables
    cos, sin = rope
    col = lambda off: (lambda b, hh: (b, off // R_QK_DIM + hh))
    per_head = lambda shape: pl.BlockSpec((None,) + shape, lambda b, hh: (hh, 0, 0))
    return pl.pallas_call(
        _ret_prompt_body,
        grid=(BATCH, R_HEADS),
        in_specs=[
            pl.BlockSpec((SEQ, R_QK_DIM), col(OFF_QR)),
            pl.BlockSpec((SEQ, R_QK_DIM), col(OFF_KR)),
            pl.BlockSpec((SEQ, R_V_DIM), col(OFF_VR)),
            pl.BlockSpec((SEQ, R_V_DIM), col(OFF_GR)),
            pl.BlockSpec((SEQ, R_QK_DIM // 2), lambda b, hh: (0, 0)),
            pl.BlockSpec((SEQ, R_QK_DIM // 2), lambda b, hh: (0, 0)),
            per_head((R_CHUNK, R_CHUNK)),
            per_head((R_CHUNK, R_QK_DIM)),
            per_head((R_CHUNK, R_QK_DIM)),
            per_head((R_QK_DIM, R_V_DIM)),
        ],
        out_specs=[
            pl.BlockSpec((SEQ, R_V_DIM), lambda b, hh: (b, hh)),
            pl.BlockSpec((None, None, R_QK_DIM, R_V_DIM), lambda b, hh: (b, hh, 0, 0)),
        ],
        out_shape=[
            jax.ShapeDtypeStruct((N_PROMPT, R_WIDTH), BF16),
            jax.ShapeDtypeStruct((BATCH, R_HEADS, R_QK_DIM, R_V_DIM), F32),
        ],
        scratch_shapes=[pltpu.VMEM((R_QK_DIM, R_V_DIM), F32)],
        compiler_params=_cparams(("parallel", "parallel"), 48),
        name="ret_prompt",
    )(h, h, h, h, cos, sin, dmask, qdec, kdec, cdec)


def _sample_attn_body(*refs, n_alias):
    q_ref, kn_ref, vn_ref, ck_ref, ckn_ref, cv_ref, cvn_ref = refs[:7]
    o_ref, ok_ref, ov_ref, m_scr, l_scr, acc_scr = refs[7 + n_alias:]
    c = pl.program_id(1)
    last = pl.num_programs(1) - 1
    scale = A_HEAD_DIM ** -0.5
    q = q_ref[...]

    @pl.when(c == 0)
    def _():
        m_scr[...] = jnp.full_like(m_scr, NEG)
        l_scr[...] = jnp.zeros_like(l_scr)
        acc_scr[...] = jnp.zeros_like(acc_scr)

    def accumulate(kr, vr, mult):
        s = jnp.sum(kr * q[None], axis=-1, keepdims=True) * scale
        m_old = m_scr[...]
        m_new = jnp.maximum(m_old, jnp.max(s, axis=0))
        alpha = jnp.exp(m_old - m_new)
        pr = jnp.exp(s - m_new[None])
        l_scr[...] = alpha * l_scr[...] + mult * jnp.sum(pr, axis=0)
        acc_scr[...] = alpha * acc_scr[...] + mult * jnp.sum(pr * vr, axis=0)
        m_scr[...] = m_new

    accumulate(ck_ref[pl.ds(0, SAMPLE_CH // 16, stride=16)], cv_ref[pl.ds(0, SAMPLE_CH // 16, stride=16)], 1.0)

    @pl.when(c == last)
    def _():
        accumulate(ck_ref[pl.ds(0, QB, stride=4)], cv_ref[pl.ds(0, QB, stride=4)], 1.0)
        accumulate(ck_ref[pl.ds(SAMPLE_CH - QB, QB)], cv_ref[pl.ds(SAMPLE_CH - QB, QB)], 1.0)
        accumulate(kn_ref[...][None], vn_ref[...][None], float(len(DILATED_PATTERNS)))
        o_ref[...] = acc_scr[...] / l_scr[...]

    ok_ref[pl.ds(0, SAMPLE_CH - 1)] = ck_ref[pl.ds(1, SAMPLE_CH - 1)]
    ov_ref[pl.ds(0, SAMPLE_CH - 1)] = cv_ref[pl.ds(1, SAMPLE_CH - 1)]
    is_last = c == last
    ok_ref[SAMPLE_CH - 1] = jnp.where(is_last, kn_ref[...], ckn_ref[0])
    ov_ref[SAMPLE_CH - 1] = jnp.where(is_last, vn_ref[...], cvn_ref[0])


def _sample_attn(q_s, k_s, v_s, cache_k, cache_v, layer, prev_k=None, prev_v=None):
    assert PAST_LEN == W_MAX and W_MAX % SAMPLE_CH == 0 and SAMPLE_CH == 512
    nc = W_MAX // SAMPLE_CH
    row = pl.BlockSpec((None, A_HEADS, A_HEAD_DIM), lambda b, c: (b, 0, 0))
    chunk = pl.BlockSpec((None, None, SAMPLE_CH, A_HEADS, A_HEAD_DIM), lambda b, c: (layer, b, c, 0, 0))
    nxt = pl.BlockSpec((None, None, 1, A_HEADS, A_HEAD_DIM),
                       lambda b, c: (layer, b, jnp.minimum((c + 1) * SAMPLE_CH, W_MAX - 1), 0, 0))
    n_alias = 0 if prev_k is None else 2
    alias_specs = [pl.BlockSpec(memory_space=pl.ANY)] * n_alias
    alias_args = [] if prev_k is None else [prev_k, prev_v]
    win_shape = jax.ShapeDtypeStruct((DEPTH, DEC_BATCH, W_MAX, A_HEADS, A_HEAD_DIM), F32)
    return pl.pallas_call(
        functools.partial(_sample_attn_body, n_alias=n_alias),
        grid=(DEC_BATCH, nc),
        in_specs=[row, row, row, chunk, nxt, chunk, nxt] + alias_specs,
        out_specs=[row, chunk, chunk],
        out_shape=[jax.ShapeDtypeStruct((DEC_BATCH, A_HEADS, A_HEAD_DIM), F32), win_shape, win_shape],
        scratch_shapes=[pltpu.VMEM((A_HEADS, A_HEAD_DIM), F32)] * 3,
        input_output_aliases={7: 1, 8: 2} if n_alias else {},
        compiler_params=_cparams(("parallel", "arbitrary"), 40),
        name="sample_attn",
    )(q_s, k_s, v_s, cache_k, cache_k, cache_v, cache_v, *alias_args)


def _sample_ret_body(*refs, n_alias, gammas):
    q_ref, k_ref, v_ref, g_ref, cos_ref, sin_ref, eye_ref, st_ref = refs[:8]
    o_ref, sto_ref = refs[8 + n_alias:]
    cos = cos_ref[...]
    sin = sin_ref[...]
    eye = eye_ref[...]
    outs = []
    for hh in range(R_HEADS):
        cols = slice(hh * R_QK_DIM, (hh + 1) * R_QK_DIM)
        q = _rotate(q_ref[:, cols], cos, sin)
        k = _rotate(k_ref[:, cols], cos, sin) * (R_QK_DIM ** -0.5)
        v = v_ref[:, cols]
        st = st_ref[hh]
        inner = jnp.sum(q * k, axis=-1, keepdims=True)
        q_dec = jnp.broadcast_to(q * gammas[hh], (8, R_QK_DIM)).astype(BF16)
        o = inner * v + _dot(q_dec, st.astype(BF16))[0:1]
        k_col = _dot_nt(eye, jnp.broadcast_to(k, (R_QK_DIM, R_QK_DIM)).astype(BF16))
        sto_ref[hh] = gammas[hh] * st + k_col * v
        outs.append(_group_norm_gate(o, g_ref[:, cols]))
    o_ref[...] = jnp.concatenate(outs, axis=-1)


def _sample_ret(q_s, k_s, v_s, g_s, state, layer, rope_s, eye, gammas, prev=None):
    row = lambda w: pl.BlockSpec((None, 1, w), lambda b: (b, 0, 0))
    const = lambda shape: pl.BlockSpec(shape, lambda b: (0,) * len(shape))
    st_spec = pl.BlockSpec((None, None, R_HEADS, R_QK_DIM, R_V_DIM), lambda b: (layer, b, 0, 0, 0))
    n_alias = 0 if prev is None else 1
    return pl.pallas_call(
        functools.partial(_sample_ret_body, n_alias=n_alias, gammas=gammas),
        grid=(DEC_BATCH,),
        in_specs=[row(R_HEADS * R_QK_DIM), row(R_HEADS * R_QK_DIM), row(R_WIDTH), row(R_WIDTH),
                  const((1, R_QK_DIM // 2)), const((1, R_QK_DIM // 2)), const((R_QK_DIM, R_QK_DIM)), st_spec]
        + [pl.BlockSpec(memory_space=pl.ANY)] * n_alias,
        out_specs=[row(R_WIDTH), st_spec],
        out_shape=[jax.ShapeDtypeStruct((DEC_BATCH, 1, R_WIDTH), F32),
                   jax.ShapeDtypeStruct((DEPTH, DEC_BATCH, R_HEADS, R_QK_DIM, R_V_DIM), F32)],
        input_output_aliases={8: 1} if n_alias else {},
        compiler_params=_cparams(("parallel",), 32),
        name="sample_ret",
    )(q_s, k_s, v_s, g_s, rope_s[0], rope_s[1], eye, state, *([] if prev is None else [prev]))


def _layer_norm(y, g, b):
    mu = jnp.mean(y, axis=-1, keepdims=True)
    var = jnp.mean(jnp.square(y - mu), axis=-1, keepdims=True)
    return (y - mu) * lax.rsqrt(var + LN_EPS) * g + b


def _split_bf16(x):
    hi = x.astype(BF16)
    lo = (x - hi.astype(F32)).astype(BF16)
    return hi, lo


def _outproj_body(attn_ref, ret_ref, x_ref, w_ref, g_ref, b_ref, wr_ref, br_ref,
                  h_ref, rdst_ref, rw_ref, cnt_ref, run_scr):
    @pl.when(pl.program_id(0) == 0)
    def _():
        run_scr[...] = jnp.zeros_like(run_scr)

    acc = _dot(attn_ref[...], w_ref[0:A_WIDTH, :]) + _dot(ret_ref[...], w_ref[A_WIDTH:, :])
    h = _layer_norm(ALPHA * x_ref[...] + acc, g_ref[...], b_ref[...])
    h_ref[...] = h

    h_hi, h_lo = _split_bf16(h)
    w_hi, w_lo = _split_bf16(wr_ref[...])
    logits = _dot(h_hi, w_hi) + _dot(h_hi, w_lo) + _dot(h_lo, w_hi) + br_ref[...]
    lane = lax.broadcasted_iota(jnp.int32, logits.shape, 1)
    valid = lane < N_EXPERTS
    logits = jnp.where(valid, logits, NEG)
    e = jnp.exp(logits - jnp.max(logits, axis=-1, keepdims=True))
    probs = e / jnp.sum(e, axis=-1, keepdims=True)
    grp = lane // EXPERTS_PER_GROUP
    best = jnp.max(jnp.where(grp == 0, probs, -1.0), axis=-1, keepdims=True)
    sel = jnp.zeros_like(best, dtype=jnp.int32)
    for gi in range(1, N_GROUPS):
        gm = jnp.max(jnp.where(grp == gi, probs, -1.0), axis=-1, keepdims=True)
        better = gm > best
        sel = jnp.where(better, gi, sel)
        best = jnp.maximum(best, gm)
    pv = jnp.where(grp == sel, probs, -1.0)
    v1 = jnp.max(pv, axis=-1, keepdims=True)
    i1 = jnp.min(jnp.where(pv == v1, lane, LANES), axis=-1, keepdims=True)
    pv2 = jnp.where(lane == i1, -1.0, pv)
    v2 = jnp.max(pv2, axis=-1, keepdims=True)
    i2 = jnp.min(jnp.where(pv2 == v2, lane, LANES), axis=-1, keepdims=True)
    tot = v1 + v2
    rw_ref[...] = jnp.where(lane == 0, v1 / tot, jnp.where(lane == 1, v2 / tot, 0.0))

    pick1 = lane == i1
    pick2 = lane == i2
    member = jnp.where(pick1 | pick2, 1.0, 0.0)
    ri = lax.broadcasted_iota(jnp.int32, (ROW_TM, ROW_TM), 0)
    ci = lax.broadcasted_iota(jnp.int32, (ROW_TM, ROW_TM), 1)
    earlier = jnp.where(ci < ri, 1.0, 0.0).astype(BF16)
    run = run_scr[...]
    rank = _dot(earlier, member.astype(BF16)) + run
    rank1 = jnp.sum(jnp.where(pick1, rank, 0.0), axis=-1, keepdims=True).astype(jnp.int32)
    rank2 = jnp.sum(jnp.where(pick2, rank, 0.0), axis=-1, keepdims=True).astype(jnp.int32)
    rdst_ref[...] = jnp.where(lane == 0, i1 * MOE_CAP + rank1, jnp.where(lane == 1, i2 * MOE_CAP + rank2, 0))
    run = run + jnp.sum(member, axis=0, keepdims=True)
    run_scr[...] = run
    cnt_ref[...] = jnp.broadcast_to(run, cnt_ref.shape).astype(jnp.int32)


def _outproj(attn, ret, x, w_out_b, ln_g, ln_b, w_router_p, b_router_p, layer):
    rows = lambda w: pl.BlockSpec((ROW_TM, w), lambda i: (i, 0))
    vec = lambda: pl.BlockSpec((None, 1, D_MODEL), lambda i: (layer, 0, 0))
    return pl.pallas_call(
        _outproj_body,
        grid=(N_TOK // ROW_TM,),
        in_specs=[rows(A_WIDTH), rows(R_WIDTH), rows(D_MODEL),
                  pl.BlockSpec((None, A_WIDTH + R_WIDTH, D_MODEL), lambda i: (layer, 0, 0)),
                  vec(), vec(),
                  pl.BlockSpec((D_MODEL, LANES), lambda i: (0, 0)),
                  pl.BlockSpec((1, LANES), lambda i: (0, 0))],
        out_specs=[rows(D_MODEL), rows(LANES), rows(LANES), pl.BlockSpec((8, LANES), lambda i: (0, 0))],
        out_shape=[jax.ShapeDtypeStruct((N_TOK, D_MODEL), F32),
                   jax.ShapeDtypeStruct((N_TOK, LANES), jnp.int32),
                   jax.ShapeDtypeStruct((N_TOK, LANES), F32),
                   jax.ShapeDtypeStruct((8, LANES), jnp.int32)],
        scratch_shapes=[pltpu.VMEM((1, LANES), F32)],
        compiler_params=_cparams(("arbitrary",), 48),
        name="outproj_ln_router",
    )(attn, ret, x, w_out_b, ln_g, ln_b, w_router_p, b_router_p)


def _block_map(counts):
    nblk = (counts + MOE_RB - 1) // MOE_RB
    cum = jnp.cumsum(nblk)
    n_used = cum[-1]
    j = jnp.minimum(jnp.arange(MOE_NB, dtype=jnp.int32), n_used - 1)
    e = jnp.minimum(jnp.sum(j[:, None] >= cum[None, :], axis=1), N_EXPERTS - 1).astype(jnp.int32)
    first = (cum - nblk)[e]
    blk_row = e * MOE_CAP_BLOCKS + (j - first)
    return blk_row.astype(jnp.int32), e, n_used.astype(jnp.int32).reshape(1)


def _row_copy(src, src_row, dst, dst_row, sem):
    return pltpu.make_async_copy(src.at[pl.ds(src_row, 1), :], dst.at[pl.ds(dst_row, 1), :], sem)


def _dispatch_body(dst_ref, cnt_ref, h_ref, xs_hbm, zbuf, zsem, sem):
    i = pl.program_id(0)

    @pl.when(i == 0)
    def _():
        zbuf[...] = jnp.zeros_like(zbuf)

        def tail_copy(e):
            nblk = (cnt_ref[e] + MOE_RB - 1) // MOE_RB
            row0 = pl.multiple_of((e * MOE_CAP_BLOCKS + nblk - 1) * MOE_RB, MOE_RB)
            return pltpu.make_async_copy(zbuf, xs_hbm.at[pl.ds(row0, MOE_RB), :], zsem)

        for e in range(N_EXPERTS):
            @pl.when(cnt_ref[e] > 0)
            def _(e=e):
                tail_copy(e).start()
        for e in range(N_EXPERTS):
            @pl.when(cnt_ref[e] > 0)
            def _(e=e):
                tail_copy(e).wait()

    base = i * ROW_TM

    def scatter_row(r, carry):
        a = TOP_K * (base + r)
        _row_copy(h_ref, r, xs_hbm, dst_ref[a], sem).start()
        _row_copy(h_ref, r, xs_hbm, dst_ref[a + 1], sem).start()
        return carry

    lax.fori_loop(0, ROW_TM, scatter_row, 0, unroll=DMA_UNROLL)
    step_rows = xs_hbm.at[pl.ds(0, TOP_K * ROW_TM), :]
    pltpu.make_async_copy(step_rows, step_rows, sem).wait()


def _dispatch(h, dst, counts):
    return pl.pallas_call(
        _dispatch_body,
        grid_spec=pltpu.PrefetchScalarGridSpec(
            num_scalar_prefetch=2,
            grid=(N_TOK // ROW_TM,),
            in_specs=[pl.BlockSpec((ROW_TM, D_MODEL), lambda i, ds, ct: (i, 0))],
            out_specs=pl.BlockSpec(memory_space=pl.ANY),
            scratch_shapes=[pltpu.VMEM((MOE_RB, D_MODEL), F32),
                            pltpu.SemaphoreType.DMA(()), pltpu.SemaphoreType.DMA(())],
        ),
        out_shape=jax.ShapeDtypeStruct((N_EXPERTS * MOE_CAP, D_MODEL), F32),
        compiler_params=_cparams(("arbitrary",), 16),
        name="moe_dispatch",
    )(dst, counts, h)


def _experts_body(br_ref, be_ref, nu_ref, x_ref, wgu_ref, wd_ref, y_ref):
    @pl.when(pl.program_id(0) < nu_ref[0])
    def _():
        gu = _dot(x_ref[...].astype(BF16), wgu_ref[...])
        gate = gu[:, :D_EXPERT]
        act = gate * jax.nn.sigmoid(gate) * gu[:, D_EXPERT:]
        y_ref[...] = _dot(act.astype(BF16), wd_ref[...])


def _experts(xs, block_map, w_gate_up_b, w_down_b, layer):
    blk_row, block_e, n_used = block_map
    rows = pl.BlockSpec((MOE_RB, D_MODEL), lambda j, br, be, nu: (br[j], 0))
    return pl.pallas_call(
        _experts_body,
        grid_spec=pltpu.PrefetchScalarGridSpec(
            num_scalar_prefetch=3,
            grid=(MOE_NB,),
            in_specs=[
                rows,
                pl.BlockSpec((None, None, D_MODEL, 2 * D_EXPERT), lambda j, br, be, nu: (layer, be[j], 0, 0)),
                pl.BlockSpec((None, None, D_EXPERT, D_MODEL), lambda j, br, be, nu: (layer, be[j], 0, 0)),
            ],
            out_specs=rows,
        ),
        out_shape=jax.ShapeDtypeStruct((N_EXPERTS * MOE_CAP, D_MODEL), F32),
        compiler_params=_cparams(("arbitrary",), 48),
        name="moe_experts",
    )(blk_row, block_e, n_used, xs, w_gate_up_b, w_down_b)


def _post_body(dst_ref, h_ref, rw_ref, p_ref, wg_ref, wp_ref, g_ref, b_ref, bg_ref, ys_hbm,
               x_ref, xb_ref, ybuf, sem):
    base = pl.program_id(0) * ROW_TM

    def gather_row(r, carry):
        a = TOP_K * (base + r)
        _row_copy(ys_hbm, dst_ref[a], ybuf, r, sem).start()
        _row_copy(ys_hbm, dst_ref[a + 1], ybuf, ROW_TM + r, sem).start()
        return carry

    lax.fori_loop(0, ROW_TM, gather_row, 0, unroll=DMA_UNROLL)
    pe = _dot(p_ref[...].astype(BF16), wp_ref[...].astype(BF16))
    pltpu.make_async_copy(ys_hbm.at[pl.ds(0, TOP_K * ROW_TM), :], ybuf, sem).wait()

    rw = rw_ref[...]
    ff = ybuf[0:ROW_TM, :] * rw[:, 0:1] + ybuf[ROW_TM:, :] * rw[:, 1:2]
    h = _layer_norm(ALPHA * h_ref[...] + ff, g_ref[...], b_ref[...])
    gate = jax.nn.sigmoid(_dot(h.astype(BF16), wg_ref[...]) + bg_ref[...])
    x = h + gate * pe
    x_ref[...] = x
    xb_ref[...] = x.astype(BF16)


def _post(h, ys, dst, rw, p, w_ple_gate_b, w_ple_proj, ln_g, ln_b, b_gate, layer):
    rows = lambda w: pl.BlockSpec((ROW_TM, w), lambda i, ds: (i, 0))
    vec = lambda: pl.BlockSpec((None, 1, D_MODEL), lambda i, ds: (layer, 0, 0))
    return pl.pallas_call(
        _post_body,
        grid_spec=pltpu.PrefetchScalarGridSpec(
            num_scalar_prefetch=1,
            grid=(N_TOK // ROW_TM,),
            in_specs=[rows(D_MODEL), rows(LANES),
                      pl.BlockSpec((None, ROW_TM, PLE_DIM), lambda i, ds: (layer, i, 0)),
                      pl.BlockSpec((None, D_MODEL, D_MODEL), lambda i, ds: (layer, 0, 0)),
                      pl.BlockSpec((None, PLE_DIM, D_MODEL), lambda i, ds: (layer, 0, 0)),
                      vec(), vec(), vec(),
                      pl.BlockSpec(memory_space=pl.ANY)],
            out_specs=[rows(D_MODEL), rows(D_MODEL)],
            scratch_shapes=[pltpu.VMEM((TOP_K * ROW_TM, D_MODEL), F32), pltpu.SemaphoreType.DMA(())],
        ),
        out_shape=[jax.ShapeDtypeStruct((N_TOK, D_MODEL), F32),
                   jax.ShapeDtypeStruct((N_TOK, D_MODEL), BF16)],
        compiler_params=_cparams(("arbitrary",), 56),
        name="combine_ln_ple",
    )(dst, h, rw, p, w_ple_gate_b, w_ple_proj, ln_g, ln_b, b_gate, ys)


def kernel(x_prompt, x_sample, cache_win_k, cache_win_v, state_ret, p_prompt, p_sample,
           w_in, w_out, ln1_g, ln1_b, ln2_g, ln2_b, w_router, b_router,
           w_gate_up, w_down, w_ple_proj, w_ple_gate, b_ple_gate):
    x = jnp.concatenate([x_prompt.reshape(N_PROMPT, D_MODEL), x_sample.reshape(DEC_BATCH, D_MODEL)], axis=0)
    xb = x.astype(BF16)
    p_all = jnp.concatenate([p_prompt.reshape(DEPTH, N_PROMPT, PLE_DIM),
                             p_sample.reshape(DEPTH, DEC_BATCH, PLE_DIM)], axis=1)
    w_out_b = w_out.astype(BF16)
    w_gate_up_b = w_gate_up.astype(BF16)
    w_down_b = w_down.astype(BF16)
    w_ple_gate_b = w_ple_gate.astype(BF16)
    w_router_p = jnp.pad(w_router, ((0, 0), (0, LANES - N_EXPERTS)))
    b_router_p = jnp.pad(b_router, (0, LANES - N_EXPERTS)).reshape(1, LANES)
    vec3 = lambda t: t.reshape(DEPTH, 1, D_MODEL)
    ln1_g, ln1_b, ln2_g, ln2_b, b_ple_gate = map(vec3, (ln1_g, ln1_b, ln2_g, ln2_b, b_ple_gate))

    tables = _retention_tables()
    gammas = tuple(float(1.0 - 2.0 ** (-5.0 - hh)) for hh in range(R_HEADS))
    rope_p = _rope_tables(jnp.arange(SEQ, dtype=F32))
    rope_s = _rope_tables(PAST_LEN + jnp.arange(1, dtype=F32))
    eye = jnp.eye(R_QK_DIM, dtype=BF16)

    win_k = win_v = st_s = None
    pk, pv, ps = [], [], []
    for layer in range(DEPTH):
        h = _inproj(xb, w_in, layer)
        attn_p = _attn_prompt(h)
        ret_p, st_p = _ret_prompt(h, tables, rope_p)
        hs = h[N_PROMPT:]
        heads = lambda off: hs[:, off:off + A_WIDTH].reshape(DEC_BATCH, A_HEADS, A_HEAD_DIM)
        attn_s, win_k, win_v = _sample_attn(heads(OFF_QA), heads(OFF_KA), heads(OFF_VA),
                                            cache_win_k, cache_win_v, layer, win_k, win_v)
        wide = lambda off, w: hs[:, off:off + w].reshape(DEC_BATCH, 1, w)
        ret_s, st_s = _sample_ret(wide(OFF_QR, R_HEADS * R_QK_DIM), wide(OFF_KR, R_HEADS * R_QK_DIM),
                                  wide(OFF_VR, R_WIDTH), wide(OFF_GR, R_WIDTH),
                                  state_ret, layer, rope_s, eye, gammas, st_s)
        attn = jnp.concatenate([attn_p, attn_s.reshape(DEC_BATCH, A_WIDTH).astype(BF16)], axis=0)
        ret = jnp.concatenate([ret_p, ret_s.reshape(DEC_BATCH, R_WIDTH).astype(BF16)], axis=0)
        h1, rdst, rw, cnt = _outproj(attn, ret, x, w_out_b, ln1_g, ln1_b, w_router_p, b_router_p, layer)
        dst = rdst[:, :TOP_K].reshape(N_ASSIGN)
        counts = cnt[0, :N_EXPERTS]
        xs = _dispatch(h1, dst, counts)
        ys = _experts(xs, _block_map(counts), w_gate_up_b, w_down_b, layer)
        x, xb = _post(h1, ys, dst, rw, p_all, w_ple_gate_b, w_ple_proj, ln2_g, ln2_b, b_ple_gate, layer)
        pk.append(h[:N_PROMPT, OFF_KA:OFF_KA + A_WIDTH].reshape(BATCH, SEQ, A_HEADS, A_HEAD_DIM))
        pv.append(h[:N_PROMPT, OFF_VA:OFF_VA + A_WIDTH].reshape(BATCH, SEQ, A_HEADS, A_HEAD_DIM))
        ps.append(st_p)

    y_prompt = x[:N_PROMPT].reshape(BATCH, SEQ, D_MODEL)
    y_sample = x[N_PROMPT:].reshape(DEC_BATCH, 1, D_MODEL)
    return (y_prompt, y_sample, jnp.stack(pk), jnp.stack(pv), jnp.stack(ps), win_k, win_v, st_s)
```

```python
import functools

import jax
import jax.numpy as jnp
from jax import lax
from jax.experimental import pallas as pl
from jax.experimental.pallas import tpu as pltpu

F32 = jnp.float32
BF16 = jnp.bfloat16

D_MODEL = 2048
BATCH = 4
SEQ = 2048
DEPTH = 2
DEC_BATCH = 128
PAST_LEN = 2048
N_PROMPT = BATCH * SEQ
N_TOK = N_PROMPT + DEC_BATCH

A_HEADS = 8
A_HEAD_DIM = 128
A_WIDTH = A_HEADS * A_HEAD_DIM
DILATED_PATTERNS = ((128, 1), (512, 4), (2048, 16))
W_MAX = 2048
QB = 128
R_HEADS = 4
R_QK_DIM = 256
R_V_DIM = 256
R_WIDTH = R_HEADS * R_V_DIM
R_CHUNK = 128
ROPE_BASE = 10000.0
OFF_QA = 0
OFF_KA = A_WIDTH
OFF_VA = 2 * A_WIDTH
OFF_QR = 3 * A_WIDTH
OFF_KR = OFF_QR + R_HEADS * R_QK_DIM
OFF_VR = OFF_KR + R_HEADS * R_QK_DIM
OFF_GR = OFF_VR + R_WIDTH
IN_COLS = OFF_GR + R_WIDTH
N_EXPERTS = 16
N_GROUPS = 4
EXPERTS_PER_GROUP = N_EXPERTS // N_GROUPS
TOP_K = 2
D_EXPERT = 1024
PLE_DIM = 256
ALPHA = (2 * DEPTH) ** 0.25
LN_EPS = 1e-5
NEG = -1e30

LANES = 128
MIB = 1024 * 1024

INPROJ_TM = 1664
INPROJ_TN = 512
ROW_TM = 320
MOE_RB = 256
N_ASSIGN = N_TOK * TOP_K
MOE_NB = (N_ASSIGN + N_EXPERTS * (MOE_RB - 1) + MOE_RB - 1) // MOE_RB
MOE_CAP = (N_TOK + MOE_RB - 1) // MOE_RB * MOE_RB
MOE_CAP_BLOCKS = MOE_CAP // MOE_RB
SAMPLE_CH = 1024
SAMPLE_RET_BB = 2
ATTN_UNROLL = 16
DMA_UNROLL = 8


def _cparams(sem, vmem_mib):
    return pltpu.CompilerParams(dimension_semantics=sem, vmem_limit_bytes=vmem_mib * MIB)


def _dot(a, b):
    return jnp.dot(a, b, preferred_element_type=F32)


def _dot_nt(a, b):
    return lax.dot_general(a, b, (((1,), (1,)), ((), ())), preferred_element_type=F32)


def _dot_tn(a, b):
    return lax.dot_general(a, b, (((0,), (0,)), ((), ())), preferred_element_type=F32)


def _inproj_body(x_ref, w_ref, o_ref):
    o_ref[...] = _dot(x_ref[...], w_ref[...].astype(BF16))


def _inproj(xb, w_in, layer):
    return pl.pallas_call(
        _inproj_body,
        grid=(N_TOK // INPROJ_TM, IN_COLS // INPROJ_TN),
        in_specs=[
            pl.BlockSpec((INPROJ_TM, D_MODEL), lambda i, j: (i, 0)),
            pl.BlockSpec((None, D_MODEL, INPROJ_TN), lambda i, j: (layer, 0, j)),
        ],
        out_specs=pl.BlockSpec((INPROJ_TM, INPROJ_TN), lambda i, j: (i, j)),
        out_shape=jax.ShapeDtypeStruct((N_TOK, IN_COLS), F32),
        compiler_params=_cparams(("parallel", "arbitrary"), 48),
        name="inproj",
    )(xb, w_in)


def _attn_prompt_body(q_ref, k_ref, v_ref, o_ref, o_scr, lse_scr):
    scale = A_HEAD_DIM ** -0.5
    qi = lax.broadcasted_iota(jnp.int32, (QB, QB), 0)
    ki = lax.broadcasted_iota(jnp.int32, (QB, QB), 1)
    cur_ok = qi >= ki
    prev_ok = ki >= qi

    for p, (window, dil) in enumerate(DILATED_PATTERNS):
        assert window // dil == QB
        nb = (SEQ // dil) // QB

        def body(it, carry, p=p, dil=dil, nb=nb):
            r = it // nb
            n = it % nb
            start = r + n * (QB * dil)
            rows = pl.ds(start, QB, stride=dil) if dil > 1 else pl.ds(pl.multiple_of(start, QB), QB)
            q = q_ref[rows, :].astype(BF16)
            k = k_ref[rows, :].astype(BF16)
            v = v_ref[rows, :].astype(BF16)
            ok = cur_ok
            if nb > 1:
                pstart = r + jnp.maximum(n - 1, 0) * (QB * dil)
                prow = (pl.ds(pstart, QB, stride=dil) if dil > 1
                        else pl.ds(pl.multiple_of(pstart, QB), QB))
                k = jnp.concatenate([k_ref[prow, :].astype(BF16), k], axis=0)
                v = jnp.concatenate([v_ref[prow, :].astype(BF16), v], axis=0)
                ok = jnp.concatenate([prev_ok & (n > 0), cur_ok], axis=1)
            s = jnp.where(ok, _dot_nt(q, k) * scale, NEG)
            m = jnp.max(s, axis=-1, keepdims=True)
            pr = jnp.exp(s - m)
            l = jnp.sum(pr, axis=-1, keepdims=True)
            acc = _dot(pr.astype(BF16), v)
            o_scr[p, rows, :] = acc / l
            lse_scr[p, rows, :] = jnp.broadcast_to(m + jnp.log(l), (QB, LANES))
            return carry

        lax.fori_loop(0, dil * nb, body, 0, unroll=ATTN_UNROLL)

    def mix(c, carry):
        rows = pl.ds(pl.multiple_of(c * QB, QB), QB)
        l0 = lse_scr[0, rows, :]
        l1 = lse_scr[1, rows, :]
        l2 = lse_scr[2, rows, :]
        mx = jnp.maximum(jnp.maximum(l0, l1), l2)
        w0 = jnp.exp(l0 - mx)
        w1 = jnp.exp(l1 - mx)
        w2 = jnp.exp(l2 - mx)
        num = w0 * o_scr[0, rows, :] + w1 * o_scr[1, rows, :] + w2 * o_scr[2, rows, :]
        o_ref[rows, :] = (num / (w0 + w1 + w2)).astype(o_ref.dtype)
        return carry

    lax.fori_loop(0, SEQ // QB, mix, 0)


def _attn_prompt(h):
    col = lambda off: (lambda b, hh: (b, off // A_HEAD_DIM + hh))
    return pl.pallas_call(
        _attn_prompt_body,
        grid=(BATCH, A_HEADS),
        in_specs=[
            pl.BlockSpec((SEQ, A_HEAD_DIM), col(OFF_QA)),
            pl.BlockSpec((SEQ, A_HEAD_DIM), col(OFF_KA)),
            pl.BlockSpec((SEQ, A_HEAD_DIM), col(OFF_VA)),
        ],
        out_specs=pl.BlockSpec((SEQ, A_HEAD_DIM), lambda b, hh: (b, hh)),
        out_shape=jax.ShapeDtypeStruct((N_PROMPT, A_WIDTH), BF16),
        scratch_shapes=[
            pltpu.VMEM((len(DILATED_PATTERNS), SEQ, A_HEAD_DIM), F32),
            pltpu.VMEM((len(DILATED_PATTERNS), SEQ, LANES), F32),
        ],
        compiler_params=_cparams(("parallel", "parallel"), 32),
        name="attn_prompt",
    )(h, h, h)


def _rotate(x, cos, sin):
    half = x.shape[-1] // 2
    x1 = x[:, :half]
    x2 = x[:, half:]
    return jnp.concatenate([x1 * cos - x2 * sin, x1 * sin + x2 * cos], axis=-1)


def _group_norm_gate(o, g):
    mu = jnp.mean(o, axis=-1, keepdims=True)
    var = jnp.mean(jnp.square(o - mu), axis=-1, keepdims=True)
    on = (o - mu) * lax.rsqrt(var + LN_EPS)
    return on * (g * jax.nn.sigmoid(g))


def _ret_prompt_body(q_ref, k_ref, v_ref, g_ref, cos_ref, sin_ref, dmask_ref, qdec_ref, kdec_ref,
                     cdec_ref, o_ref, st_ref, st_scr):
    st_scr[...] = jnp.zeros_like(st_scr)
    dmask = dmask_ref[...]
    qdec = qdec_ref[...]
    kdec = kdec_ref[...]
    cdec = cdec_ref[...]

    def chunk(c, carry):
        rows = pl.ds(pl.multiple_of(c * R_CHUNK, R_CHUNK), R_CHUNK)
        cos = cos_ref[rows, :]
        sin = sin_ref[rows, :]
        q = _rotate(q_ref[rows, :], cos, sin)
        k = _rotate(k_ref[rows, :], cos, sin) * (R_QK_DIM ** -0.5)
        v = v_ref[rows, :].astype(BF16)
        st = st_scr[...]
        inner = _dot_nt(q.astype(BF16), k.astype(BF16)) * dmask
        o = _dot(inner.astype(BF16), v) + _dot((q * qdec).astype(BF16), st.astype(BF16))
        st_scr[...] = cdec * st + _dot_tn((k * kdec).astype(BF16), v)
        o_ref[rows, :] = _group_norm_gate(o, g_ref[rows, :]).astype(o_ref.dtype)
        return carry

    lax.fori_loop(0, SEQ // R_CHUNK, chunk, 0, unroll=2)
    st_ref[...] = st_scr[...]


def _retention_tables():
    lg = jnp.log1p(-jnp.exp2(-5.0 - jnp.arange(R_HEADS, dtype=F32)))
    i = jnp.arange(R_CHUNK, dtype=F32)
    rel = i[:, None] - i[None, :]
    dmask = jnp.exp(jnp.where(rel[None] >= 0, rel[None] * lg[:, None, None], -jnp.inf))
    qdec = jnp.exp((i[None, :] + 1.0) * lg[:, None])
    kdec = jnp.exp((R_CHUNK - 1.0 - i)[None, :] * lg[:, None])
    cdec = jnp.exp(R_CHUNK * lg)
    bcast = lambda t: jnp.broadcast_to(t[:, :, None], (R_HEADS, R_CHUNK, R_QK_DIM))
    return (dmask, bcast(qdec), bcast(kdec),
            jnp.broadcast_to(cdec[:, None, None], (R_HEADS, R_QK_DIM, R_V_DIM)), lg)


def _rope_tables(pos):
    half = R_QK_DIM // 2
    inv = 1.0 / (ROPE_BASE ** jnp.linspace(0.0, 1.0, half, dtype=F32))
    ang = pos[:, None] * inv[None, :]
    return jnp.cos(ang), jnp.sin(ang)


def _ret_prompt(h, tables, rope):
    dmask, qdec, kdec, cdec, _ = tables
    cos, sin = rope
    col = lambda off: (lambda b, hh: (b, off // R_QK_DIM + hh))
    per_head = lambda shape: pl.BlockSpec((None,) + shape, lambda b, hh: (hh, 0, 0))
    return pl.pallas_call(
        _ret_prompt_body,
        grid=(BATCH, R_HEADS),
        in_specs=[
            pl.BlockSpec((SEQ, R_QK_DIM), col(OFF_QR)),
            pl.BlockSpec((SEQ, R_QK_DIM), col(OFF_KR)),
            pl.BlockSpec((SEQ, R_V_DIM), col(OFF_VR)),
            pl.BlockSpec((SEQ, R_V_DIM), col(OFF_GR)),
            pl.BlockSpec((SEQ, R_QK_DIM // 2), lambda b, hh: (0, 0)),
            pl.BlockSpec((SEQ, R_QK_DIM // 2), lambda b, hh: (0, 0)),
            per_head((R_CHUNK, R_CHUNK)),
            per_head((R_CHUNK, R_QK_DIM)),
            per_head((R_CHUNK, R_QK_DIM)),
            per_head((R_QK_DIM, R_V_DIM)),
        ],
        out_specs=[
            pl.BlockSpec((SEQ, R_V_DIM), lambda b, hh: (b, hh)),
            pl.BlockSpec((None, None, R_QK_DIM, R_V_DIM), lambda b, hh: (b, hh, 0, 0)),
        ],
        out_shape=[
            jax.ShapeDtypeStruct((N_PROMPT, R_WIDTH), BF16),
            jax.ShapeDtypeStruct((BATCH, R_HEADS, R_QK_DIM, R_V_DIM), F32),
        ],
        scratch_shapes=[pltpu.VMEM((R_QK_DIM, R_V_DIM), F32)],
        compiler_params=_cparams(("parallel", "parallel"), 48),
        name="ret_prompt",
    )(h, h, h, h, cos, sin, dmask, qdec, kdec, cdec)


def _sample_attn_body(*refs, n_alias):
    q_ref, kn_ref, vn_ref, ck_ref, ckn_ref, cv_ref, cvn_ref = refs[:7]
    o_ref, ok_ref, ov_ref, m_scr, l_scr, acc_scr = refs[7 + n_alias:]
    c = pl.program_id(1)
    last = pl.num_programs(1) - 1
    scale = A_HEAD_DIM ** -0.5
    q = q_ref[...]

    @pl.when(c == 0)
    def _():
        m_scr[...] = jnp.full_like(m_scr, NEG)
        l_scr[...] = jnp.zeros_like(l_scr)
        acc_scr[...] = jnp.zeros_like(acc_scr)

    def accumulate(kr, vr, mult):
        s = jnp.sum(kr * q[None], axis=-1, keepdims=True) * scale
        m_old = m_scr[...]
        m_new = jnp.maximum(m_old, jnp.max(s, axis=0))
        alpha = jnp.exp(m_old - m_new)
        pr = jnp.exp(s - m_new[None])
        l_scr[...] = alpha * l_scr[...] + mult * jnp.sum(pr, axis=0)
        acc_scr[...] = alpha * acc_scr[...] + mult * jnp.sum(pr * vr, axis=0)
        m_scr[...] = m_new

    accumulate(ck_ref[pl.ds(0, SAMPLE_CH // 16, stride=16)], cv_ref[pl.ds(0, SAMPLE_CH // 16, stride=16)], 1.0)

    @pl.when(c == last)
    def _():
        d4 = pl.ds(SAMPLE_CH - 4 * QB, QB, stride=4)
        accumulate(ck_ref[d4], cv_ref[d4], 1.0)
        accumulate(ck_ref[pl.ds(SAMPLE_CH - QB, QB)], cv_ref[pl.ds(SAMPLE_CH - QB, QB)], 1.0)
        accumulate(kn_ref[...][None], vn_ref[...][None], float(len(DILATED_PATTERNS)))
        o_ref[...] = acc_scr[...] / l_scr[...]

    ok_ref[pl.ds(0, SAMPLE_CH - 1)] = ck_ref[pl.ds(1, SAMPLE_CH - 1)]
    ov_ref[pl.ds(0, SAMPLE_CH - 1)] = cv_ref[pl.ds(1, SAMPLE_CH - 1)]
    is_last = c == last
    ok_ref[SAMPLE_CH - 1] = jnp.where(is_last, kn_ref[...], ckn_ref[0])
    ov_ref[SAMPLE_CH - 1] = jnp.where(is_last, vn_ref[...], cvn_ref[0])


def _sample_attn(q_s, k_s, v_s, cache_k, cache_v, layer, prev_k=None, prev_v=None):
    assert PAST_LEN == W_MAX and W_MAX % SAMPLE_CH == 0 and SAMPLE_CH >= 4 * QB and SAMPLE_CH % 16 == 0
    nc = W_MAX // SAMPLE_CH
    row = pl.BlockSpec((None, A_HEADS, A_HEAD_DIM), lambda b, c: (b, 0, 0))
    chunk = pl.BlockSpec((None, None, SAMPLE_CH, A_HEADS, A_HEAD_DIM), lambda b, c: (layer, b, c, 0, 0))
    nxt = pl.BlockSpec((None, None, 1, A_HEADS, A_HEAD_DIM),
                       lambda b, c: (layer, b, jnp.minimum((c + 1) * SAMPLE_CH, W_MAX - 1), 0, 0))
    n_alias = 0 if prev_k is None else 2
    alias_specs = [pl.BlockSpec(memory_space=pl.ANY)] * n_alias
    alias_args = [] if prev_k is None else [prev_k, prev_v]
    win_shape = jax.ShapeDtypeStruct((DEPTH, DEC_BATCH, W_MAX, A_HEADS, A_HEAD_DIM), F32)
    return pl.pallas_call(
        functools.partial(_sample_attn_body, n_alias=n_alias),
        grid=(DEC_BATCH, nc),
        in_specs=[row, row, row, chunk, nxt, chunk, nxt] + alias_specs,
        out_specs=[row, chunk, chunk],
        out_shape=[jax.ShapeDtypeStruct((DEC_BATCH, A_HEADS, A_HEAD_DIM), F32), win_shape, win_shape],
        scratch_shapes=[pltpu.VMEM((A_HEADS, A_HEAD_DIM), F32)] * 3,
        input_output_aliases={7: 1, 8: 2} if n_alias else {},
        compiler_params=_cparams(("parallel", "arbitrary"), 48),
        name="sample_attn",
    )(q_s, k_s, v_s, cache_k, cache_k, cache_v, cache_v, *alias_args)


def _sample_ret_body(*refs, n_alias, gammas):
    q_ref, k_ref, v_ref, g_ref, cos_ref, sin_ref, eye_ref, st_ref = refs[:8]
    o_ref, sto_ref = refs[8 + n_alias:]
    cos = cos_ref[...]
    sin = sin_ref[...]
    eye = eye_ref[...]
    for bb in range(SAMPLE_RET_BB):
        outs = []
        for hh in range(R_HEADS):
            cols = slice(hh * R_QK_DIM, (hh + 1) * R_QK_DIM)
            q = _rotate(q_ref[bb, :, cols], cos, sin)
            k = _rotate(k_ref[bb, :, cols], cos, sin) * (R_QK_DIM ** -0.5)
            v = v_ref[bb, :, cols]
            st = st_ref[bb, hh]
            inner = jnp.sum(q * k, axis=-1, keepdims=True)
            q_dec = jnp.broadcast_to(q * gammas[hh], (8, R_QK_DIM)).astype(BF16)
            o = inner * v + _dot(q_dec, st.astype(BF16))[0:1]
            k_col = _dot_nt(eye, jnp.broadcast_to(k, (R_QK_DIM, R_QK_DIM)).astype(BF16))
            sto_ref[bb, hh] = gammas[hh] * st + k_col * v
            outs.append(_group_norm_gate(o, g_ref[bb, :, cols]))
        o_ref[bb] = jnp.concatenate(outs, axis=-1)


def _sample_ret(q_s, k_s, v_s, g_s, state, layer, rope_s, eye, gammas, prev=None):
    row = lambda w: pl.BlockSpec((SAMPLE_RET_BB, 1, w), lambda b: (b, 0, 0))
    const = lambda shape: pl.BlockSpec(shape, lambda b: (0,) * len(shape))
    st_spec = pl.BlockSpec((None, SAMPLE_RET_BB, R_HEADS, R_QK_DIM, R_V_DIM), lambda b: (layer, b, 0, 0, 0))
    n_alias = 0 if prev is None else 1
    return pl.pallas_call(
        functools.partial(_sample_ret_body, n_alias=n_alias, gammas=gammas),
        grid=(DEC_BATCH // SAMPLE_RET_BB,),
        in_specs=[row(R_HEADS * R_QK_DIM), row(R_HEADS * R_QK_DIM), row(R_WIDTH), row(R_WIDTH),
                  const((1, R_QK_DIM // 2)), const((1, R_QK_DIM // 2)), const((R_QK_DIM, R_QK_DIM)), st_spec]
        + [pl.BlockSpec(memory_space=pl.ANY)] * n_alias,
        out_specs=[row(R_WIDTH), st_spec],
        out_shape=[jax.ShapeDtypeStruct((DEC_BATCH, 1, R_WIDTH), F32),
                   jax.ShapeDtypeStruct((DEPTH, DEC_BATCH, R_HEADS, R_QK_DIM, R_V_DIM), F32)],
        input_output_aliases={8: 1} if n_alias else {},
        compiler_params=_cparams(("parallel",), 32),
        name="sample_ret",
    )(q_s, k_s, v_s, g_s, rope_s[0], rope_s[1], eye, state, *([] if prev is None else [prev]))


def _layer_norm(y, g, b):
    mu = jnp.mean(y, axis=-1, keepdims=True)
    var = jnp.mean(jnp.square(y - mu), axis=-1, keepdims=True)
    return (y - mu) * lax.rsqrt(var + LN_EPS) * g + b


def _split_bf16(x):
    hi = x.astype(BF16)
    lo = (x - hi.astype(F32)).astype(BF16)
    return hi, lo


def _outproj_body(attn_ref, ret_ref, x_ref, w_ref, g_ref, b_ref, wr_ref, br_ref,
                  h_ref, rdst_ref, rw_ref, cnt_ref, run_scr):
    @pl.when(pl.program_id(0) == 0)
    def _():
        run_scr[...] = jnp.zeros_like(run_scr)

    acc = _dot(attn_ref[...], w_ref[0:A_WIDTH, :]) + _dot(ret_ref[...], w_ref[A_WIDTH:, :])
    h = _layer_norm(ALPHA * x_ref[...] + acc, g_ref[...], b_ref[...])
    h_ref[...] = h

    h_hi, h_lo = _split_bf16(h)
    w_hi, w_lo = _split_bf16(wr_ref[...])
    logits = _dot(h_hi, w_hi) + _dot(h_hi, w_lo) + _dot(h_lo, w_hi) + br_ref[...]
    lane = lax.broadcasted_iota(jnp.int32, logits.shape, 1)
    valid = lane < N_EXPERTS
    logits = jnp.where(valid, logits, NEG)
    e = jnp.exp(logits - jnp.max(logits, axis=-1, keepdims=True))
    probs = e / jnp.sum(e, axis=-1, keepdims=True)
    grp = lane // EXPERTS_PER_GROUP
    best = jnp.max(jnp.where(grp == 0, probs, -1.0), axis=-1, keepdims=True)
    sel = jnp.zeros_like(best, dtype=jnp.int32)
    for gi in range(1, N_GROUPS):
        gm = jnp.max(jnp.where(grp == gi, probs, -1.0), axis=-1, keepdims=True)
        better = gm > best
        sel = jnp.where(better, gi, sel)
        best = jnp.maximum(best, gm)
    pv = jnp.where(grp == sel, probs, -1.0)
    v1 = jnp.max(pv, axis=-1, keepdims=True)
    i1 = jnp.min(jnp.where(pv == v1, lane, LANES), axis=-1, keepdims=True)
    pv2 = jnp.where(lane == i1, -1.0, pv)
    v2 = jnp.max(pv2, axis=-1, keepdims=True)
    i2 = jnp.min(jnp.where(pv2 == v2, lane, LANES), axis=-1, keepdims=True)
    tot = v1 + v2
    rw_ref[...] = jnp.where(lane == 0, v1 / tot, jnp.where(lane == 1, v2 / tot, 0.0))

    pick1 = lane == i1
    pick2 = lane == i2
    member = jnp.where(pick1 | pick2, 1.0, 0.0)
    ri = lax.broadcasted_iota(jnp.int32, (ROW_TM, ROW_TM), 0)
    ci = lax.broadcasted_iota(jnp.int32, (ROW_TM, ROW_TM), 1)
    earlier = jnp.where(ci < ri, 1.0, 0.0).astype(BF16)
    run = run_scr[...]
    rank = _dot(earlier, member.astype(BF16)) + run
    rank1 = jnp.sum(jnp.where(pick1, rank, 0.0), axis=-1, keepdims=True).astype(jnp.int32)
    rank2 = jnp.sum(jnp.where(pick2, rank, 0.0), axis=-1, keepdims=True).astype(jnp.int32)
    rdst_ref[...] = jnp.where(lane == 0, i1 * MOE_CAP + rank1, jnp.where(lane == 1, i2 * MOE_CAP + rank2, 0))
    run = run + jnp.sum(member, axis=0, keepdims=True)
    run_scr[...] = run
    cnt_ref[...] = jnp.broadcast_to(run, cnt_ref.shape).astype(jnp.int32)


def _outproj(attn, ret, x, w_out_b, ln_g, ln_b, w_router_p, b_router_p, layer):
    rows = lambda w: pl.BlockSpec((ROW_TM, w), lambda i: (i, 0))
    vec = lambda: pl.BlockSpec((None, 1, D_MODEL), lambda i: (layer, 0, 0))
    return pl.pallas_call(
        _outproj_body,
        grid=(N_TOK // ROW_TM,),
        in_specs=[rows(A_WIDTH), rows(R_WIDTH), rows(D_MODEL),
                  pl.BlockSpec((None, A_WIDTH + R_WIDTH, D_MODEL), lambda i: (layer, 0, 0)),
                  vec(), vec(),
                  pl.BlockSpec((D_MODEL, LANES), lambda i: (0, 0)),
                  pl.BlockSpec((1, LANES), lambda i: (0, 0))],
        out_specs=[rows(D_MODEL), rows(LANES), rows(LANES), pl.BlockSpec((8, LANES), lambda i: (0, 0))],
        out_shape=[jax.ShapeDtypeStruct((N_TOK, D_MODEL), F32),
                   jax.ShapeDtypeStruct((N_TOK, LANES), jnp.int32),
                   jax.ShapeDtypeStruct((N_TOK, LANES), F32),
                   jax.ShapeDtypeStruct((8, LANES), jnp.int32)],
        scratch_shapes=[pltpu.VMEM((1, LANES), F32)],
        compiler_params=_cparams(("arbitrary",), 48),
        name="outproj_ln_router",
    )(attn, ret, x, w_out_b, ln_g, ln_b, w_router_p, b_router_p)


def _block_map(counts):
    nblk = (counts + MOE_RB - 1) // MOE_RB
    cum = jnp.cumsum(nblk)
    n_used = cum[-1]
    j = jnp.minimum(jnp.arange(MOE_NB, dtype=jnp.int32), n_used - 1)
    e = jnp.minimum(jnp.sum(j[:, None] >= cum[None, :], axis=1), N_EXPERTS - 1).astype(jnp.int32)
    first = (cum - nblk)[e]
    blk_row = e * MOE_CAP_BLOCKS + (j - first)
    return blk_row.astype(jnp.int32), e, n_used.astype(jnp.int32).reshape(1)


def _row_copy(src, src_row, dst, dst_row, sem):
    return pltpu.make_async_copy(src.at[pl.ds(src_row, 1), :], dst.at[pl.ds(dst_row, 1), :], sem)


def _dispatch_body(dst_ref, cnt_ref, h_ref, xs_hbm, zbuf, zsem, sem):
    i = pl.program_id(0)

    @pl.when(i == 0)
    def _():
        zbuf[...] = jnp.zeros_like(zbuf)

        def tail_copy(e):
            nblk = (cnt_ref[e] + MOE_RB - 1) // MOE_RB
            row0 = pl.multiple_of((e * MOE_CAP_BLOCKS + nblk - 1) * MOE_RB, MOE_RB)
            return pltpu.make_async_copy(zbuf, xs_hbm.at[pl.ds(row0, MOE_RB), :], zsem)

        for e in range(N_EXPERTS):
            @pl.when(cnt_ref[e] > 0)
            def _(e=e):
                tail_copy(e).start()
        for e in range(N_EXPERTS):
            @pl.when(cnt_ref[e] > 0)
            def _(e=e):
                tail_copy(e).wait()

    base = i * ROW_TM

    def scatter_row(r, carry):
        a = TOP_K * (base + r)
        _row_copy(h_ref, r, xs_hbm, dst_ref[a], sem).start()
        _row_copy(h_ref, r, xs_hbm, dst_ref[a + 1], sem).start()
        return carry

    lax.fori_loop(0, ROW_TM, scatter_row, 0, unroll=DMA_UNROLL)
    step_rows = xs_hbm.at[pl.ds(0, TOP_K * ROW_TM), :]
    pltpu.make_async_copy(step_rows, step_rows, sem).wait()


def _dispatch(h, dst, counts):
    return pl.pallas_call(
        _dispatch_body,
        grid_spec=pltpu.PrefetchScalarGridSpec(
            num_scalar_prefetch=2,
            grid=(N_TOK // ROW_TM,),
            in_specs=[pl.BlockSpec((ROW_TM, D_MODEL), lambda i, ds, ct: (i, 0))],
            out_specs=pl.BlockSpec(memory_space=pl.ANY),
            scratch_shapes=[pltpu.VMEM((MOE_RB, D_MODEL), F32),
                            pltpu.SemaphoreType.DMA(()), pltpu.SemaphoreType.DMA(())],
        ),
        out_shape=jax.ShapeDtypeStruct((N_EXPERTS * MOE_CAP, D_MODEL), F32),
        compiler_params=_cparams(("arbitrary",), 16),
        name="moe_dispatch",
    )(dst, counts, h)


def _experts_body(br_ref, be_ref, nu_ref, x_ref, wgu_ref, wd_ref, y_ref):
    @pl.when(pl.program_id(0) < nu_ref[0])
    def _():
        gu = _dot(x_ref[...].astype(BF16), wgu_ref[...])
        gate = gu[:, :D_EXPERT]
        act = gate * jax.nn.sigmoid(gate) * gu[:, D_EXPERT:]
        y_ref[...] = _dot(act.astype(BF16), wd_ref[...])


def _experts(xs, block_map, w_gate_up_b, w_down_b, layer):
    blk_row, block_e, n_used = block_map
    rows = pl.BlockSpec((MOE_RB, D_MODEL), lambda j, br, be, nu: (br[j], 0))
    return pl.pallas_call(
        _experts_body,
        grid_spec=pltpu.PrefetchScalarGridSpec(
            num_scalar_prefetch=3,
            grid=(MOE_NB,),
            in_specs=[
                rows,
                pl.BlockSpec((None, None, D_MODEL, 2 * D_EXPERT), lambda j, br, be, nu: (layer, be[j], 0, 0)),
                pl.BlockSpec((None, None, D_EXPERT, D_MODEL), lambda j, br, be, nu: (layer, be[j], 0, 0)),
            ],
            out_specs=rows,
        ),
        out_shape=jax.ShapeDtypeStruct((N_EXPERTS * MOE_CAP, D_MODEL), F32),
        compiler_params=_cparams(("arbitrary",), 48),
        name="moe_experts",
    )(blk_row, block_e, n_used, xs, w_gate_up_b, w_down_b)


def _post_body(dst_ref, h_ref, rw_ref, p_ref, wg_ref, wp_ref, g_ref, b_ref, bg_ref, ys_hbm,
               x_ref, xb_ref, ybuf, sem):
    base = pl.program_id(0) * ROW_TM

    def gather_row(r, carry):
        a = TOP_K * (base + r)
        _row_copy(ys_hbm, dst_ref[a], ybuf, r, sem).start()
        _row_copy(ys_hbm, dst_ref[a + 1], ybuf, ROW_TM + r, sem).start()
        return carry

    lax.fori_loop(0, ROW_TM, gather_row, 0, unroll=DMA_UNROLL)
    pe = _dot(p_ref[...].astype(BF16), wp_ref[...].astype(BF16))
    pltpu.make_async_copy(ys_hbm.at[pl.ds(0, TOP_K * ROW_TM), :], ybuf, sem).wait()

    rw = rw_ref[...]
    ff = ybuf[0:ROW_TM, :] * rw[:, 0:1] + ybuf[ROW_TM:, :] * rw[:, 1:2]
    h = _layer_norm(ALPHA * h_ref[...] + ff, g_ref[...], b_ref[...])
    gate = jax.nn.sigmoid(_dot(h.astype(BF16), wg_ref[...]) + bg_ref[...])
    x = h + gate * pe
    x_ref[...] = x
    xb_ref[...] = x.astype(BF16)


def _post(h, ys, dst, rw, p, w_ple_gate_b, w_ple_proj, ln_g, ln_b, b_gate, layer):
    rows = lambda w: pl.BlockSpec((ROW_TM, w), lambda i, ds: (i, 0))
    vec = lambda: pl.BlockSpec((None, 1, D_MODEL), lambda i, ds: (layer, 0, 0))
    return pl.pallas_call(
        _post_body,
        grid_spec=pltpu.PrefetchScalarGridSpec(
            num_scalar_prefetch=1,
            grid=(N_TOK // ROW_TM,),
            in_specs=[rows(D_MODEL), rows(LANES),
                      pl.BlockSpec((None, ROW_TM, PLE_DIM), lambda i, ds: (layer, i, 0)),
                      pl.BlockSpec((None, D_MODEL, D_MODEL), lambda i, ds: (layer, 0, 0)),
                      pl.BlockSpec((None, PLE_DIM, D_MODEL), lambda i, ds: (layer, 0, 0)),
                      vec(), vec(), vec(),
                      pl.BlockSpec(memory_space=pl.ANY)],
            out_specs=[rows(D_MODEL), rows(D_MODEL)],
            scratch_shapes=[pltpu.VMEM((TOP_K * ROW_TM, D_MODEL), F32), pltpu.SemaphoreType.DMA(())],
        ),
        out_shape=[jax.ShapeDtypeStruct((N_TOK, D_MODEL), F32),
                   jax.ShapeDtypeStruct((N_TOK, D_MODEL), BF16)],
        compiler_params=_cparams(("arbitrary",), 56),
        name="combine_ln_ple",
    )(dst, h, rw, p, w_ple_gate_b, w_ple_proj, ln_g, ln_b, b_gate, ys)


def kernel(x_prompt, x_sample, cache_win_k, cache_win_v, state_ret, p_prompt, p_sample,
           w_in, w_out, ln1_g, ln1_b, ln2_g, ln2_b, w_router, b_router,
           w_gate_up, w_down, w_ple_proj, w_ple_gate, b_ple_gate):
    x = jnp.concatenate([x_prompt.reshape(N_PROMPT, D_MODEL), x_sample.reshape(DEC_BATCH, D_MODEL)], axis=0)
    xb = x.astype(BF16)
    p_all = jnp.concatenate([p_prompt.reshape(DEPTH, N_PROMPT, PLE_DIM),
                             p_sample.reshape(DEPTH, DEC_BATCH, PLE_DIM)], axis=1)
    w_out_b = w_out.astype(BF16)
    w_gate_up_b = w_gate_up.astype(BF16)
    w_down_b = w_down.astype(BF16)
    w_ple_gate_b = w_ple_gate.astype(BF16)
    w_router_p = jnp.pad(w_router, ((0, 0), (0, LANES - N_EXPERTS)))
    b_router_p = jnp.pad(b_router, (0, LANES - N_EXPERTS)).reshape(1, LANES)
    vec3 = lambda t: t.reshape(DEPTH, 1, D_MODEL)
    ln1_g, ln1_b, ln2_g, ln2_b, b_ple_gate = map(vec3, (ln1_g, ln1_b, ln2_g, ln2_b, b_ple_gate))

    tables = _retention_tables()
    gammas = tuple(float(1.0 - 2.0 ** (-5.0 - hh)) for hh in range(R_HEADS))
    rope_p = _rope_tables(jnp.arange(SEQ, dtype=F32))
    rope_s = _rope_tables(PAST_LEN + jnp.arange(1, dtype=F32))
    eye = jnp.eye(R_QK_DIM, dtype=BF16)

    win_k = win_v = st_s = None
    pk, pv, ps = [], [], []
    for layer in range(DEPTH):
        h = _inproj(xb, w_in, layer)
        attn_p = _attn_prompt(h)
        ret_p, st_p = _ret_prompt(h, tables, rope_p)
        hs = h[N_PROMPT:]
        heads = lambda off: hs[:, off:off + A_WIDTH].reshape(DEC_BATCH, A_HEADS, A_HEAD_DIM)
        attn_s, win_k, win_v = _sample_attn(heads(OFF_QA), heads(OFF_KA), heads(OFF_VA),
                                            cache_win_k, cache_win_v, layer, win_k, win_v)
        wide = lambda off, w: hs[:, off:off + w].reshape(DEC_BATCH, 1, w)
        ret_s, st_s = _sample_ret(wide(OFF_QR, R_HEADS * R_QK_DIM), wide(OFF_KR, R_HEADS * R_QK_DIM),
                                  wide(OFF_VR, R_WIDTH), wide(OFF_GR, R_WIDTH),
                                  state_ret, layer, rope_s, eye, gammas, st_s)
        attn = jnp.concatenate([attn_p, attn_s.reshape(DEC_BATCH, A_WIDTH).astype(BF16)], axis=0)
        ret = jnp.concatenate([ret_p, ret_s.reshape(DEC_BATCH, R_WIDTH).astype(BF16)], axis=0)
        h1, rdst, rw, cnt = _outproj(attn, ret, x, w_out_b, ln1_g, ln1_b, w_router_p, b_router_p, layer)
        dst = rdst[:, :TOP_K].reshape(N_ASSIGN)
        counts = cnt[0, :N_EXPERTS]
        xs = _dispatch(h1, dst, counts)
        ys = _experts(xs, _block_map(counts), w_gate_up_b, w_down_b, layer)
        x, xb = _post(h1, ys, dst, rw, p_all, w_ple_gate_b, w_ple_proj, ln2_g, ln2_b, b_ple_gate, layer)
        pk.append(h[:N_PROMPT, OFF_KA:OFF_KA + A_WIDTH].reshape(BATCH, SEQ, A_HEADS, A_HEAD_DIM))
        pv.append(h[:N_PROMPT, OFF_VA:OFF_VA + A_WIDTH].reshape(BATCH, SEQ, A_HEADS, A_HEAD_DIM))
        ps.append(st_p)

    y_prompt = x[:N_PROMPT].reshape(BATCH, SEQ, D_MODEL)
    y_sample = x[N_PROMPT:].reshape(DEC_BATCH, 1, D_MODEL)
    return (y_prompt, y_sample, jnp.stack(pk), jnp.stack(pv), jnp.stack(ps), win_k, win_v, st_s)
```

```python
import functools

import jax
import jax.numpy as jnp
from jax import lax
from jax.experimental import pallas as pl
from jax.experimental.pallas import tpu as pltpu

F32 = jnp.float32
BF16 = jnp.bfloat16

D_MODEL = 2048
BATCH = 4
SEQ = 2048
DEPTH = 2
DEC_BATCH = 128
PAST_LEN = 2048
N_PROMPT = BATCH * SEQ
N_TOK = N_PROMPT + DEC_BATCH

A_HEADS = 8
A_HEAD_DIM = 128
A_WIDTH = A_HEADS * A_HEAD_DIM
DILATED_PATTERNS = ((128, 1), (512, 4), (2048, 16))
W_MAX = 2048
QB = 128
R_HEADS = 4
R_QK_DIM = 256
R_V_DIM = 256
R_WIDTH = R_HEADS * R_V_DIM
R_CHUNK = 128
ROPE_BASE = 10000.0
OFF_QA = 0
OFF_KA = A_WIDTH
OFF_VA = 2 * A_WIDTH
OFF_QR = 3 * A_WIDTH
OFF_KR = OFF_QR + R_HEADS * R_QK_DIM
OFF_VR = OFF_KR + R_HEADS * R_QK_DIM
OFF_GR = OFF_VR + R_WIDTH
IN_COLS = OFF_GR + R_WIDTH
N_EXPERTS = 16
N_GROUPS = 4
EXPERTS_PER_GROUP = N_EXPERTS // N_GROUPS
TOP_K = 2
D_EXPERT = 1024
PLE_DIM = 256
ALPHA = (2 * DEPTH) ** 0.25
LN_EPS = 1e-5
NEG = -1e30

LANES = 128
MIB = 1024 * 1024

INPROJ_TM = 1664
INPROJ_TN = 512
ROW_TM = 320
MOE_RB = 256
N_ASSIGN = N_TOK * TOP_K
MOE_NB = (N_ASSIGN + N_EXPERTS * (MOE_RB - 1) + MOE_RB - 1) // MOE_RB
MOE_CAP = (N_TOK + MOE_RB - 1) // MOE_RB * MOE_RB
MOE_CAP_BLOCKS = MOE_CAP // MOE_RB
SAMPLE_CH = 1024
SAMPLE_RET_BB = 2
ATTN_UNROLL = 16
DMA_UNROLL = 8


def _cparams(sem, vmem_mib):
    return pltpu.CompilerParams(dimension_semantics=sem, vmem_limit_bytes=vmem_mib * MIB)


def _dot(a, b):
    return jnp.dot(a, b, preferred_element_type=F32)


def _dot_nt(a, b):
    return lax.dot_general(a, b, (((1,), (1,)), ((), ())), preferred_element_type=F32)


def _dot_tn(a, b):
    return lax.dot_general(a, b, (((0,), (0,)), ((), ())), preferred_element_type=F32)


def _inproj_body(x_ref, w_ref, o_ref):
    o_ref[...] = _dot(x_ref[...], w_ref[...].astype(BF16))


def _inproj(xb, w_in, layer):
    return pl.pallas_call(
        _inproj_body,
        grid=(N_TOK // INPROJ_TM, IN_COLS // INPROJ_TN),
        in_specs=[
            pl.BlockSpec((INPROJ_TM, D_MODEL), lambda i, j: (i, 0)),
            pl.BlockSpec((None, D_MODEL, INPROJ_TN), lambda i, j: (layer, 0, j)),
        ],
        out_specs=pl.BlockSpec((INPROJ_TM, INPROJ_TN), lambda i, j: (i, j)),
        out_shape=jax.ShapeDtypeStruct((N_TOK, IN_COLS), F32),
        compiler_params=_cparams(("parallel", "arbitrary"), 48),
        name="inproj",
    )(xb, w_in)


def _attn_prompt_body(q_ref, k_ref, v_ref, o_ref, o_scr, lse_scr):
    scale = A_HEAD_DIM ** -0.5
    qi = lax.broadcasted_iota(jnp.int32, (QB, QB), 0)
    ki = lax.broadcasted_iota(jnp.int32, (QB, QB), 1)
    cur_ok = qi >= ki
    prev_ok = ki >= qi

    for p, (window, dil) in enumerate(DILATED_PATTERNS):
        assert window // dil == QB
        nb = (SEQ // dil) // QB

        def body(it, carry, p=p, dil=dil, nb=nb):
            r = it // nb
            n = it % nb
            start = r + n * (QB * dil)
            rows = pl.ds(start, QB, stride=dil) if dil > 1 else pl.ds(pl.multiple_of(start, QB), QB)
            q = q_ref[rows, :].astype(BF16)
            k = k_ref[rows, :].astype(BF16)
            v = v_ref[rows, :].astype(BF16)
            ok = cur_ok
            if nb > 1:
                pstart = r + jnp.maximum(n - 1, 0) * (QB * dil)
                prow = (pl.ds(pstart, QB, stride=dil) if dil > 1
                        else pl.ds(pl.multiple_of(pstart, QB), QB))
                k = jnp.concatenate([k_ref[prow, :].astype(BF16), k], axis=0)
                v = jnp.concatenate([v_ref[prow, :].astype(BF16), v], axis=0)
                ok = jnp.concatenate([prev_ok & (n > 0), cur_ok], axis=1)
            s = jnp.where(ok, _dot_nt(q, k) * scale, NEG)
            m = jnp.max(s, axis=-1, keepdims=True)
            pr = jnp.exp(s - m)
            l = jnp.sum(pr, axis=-1, keepdims=True)
            acc = _dot(pr.astype(BF16), v)
            o_scr[p, rows, :] = acc / l
            lse_scr[p, rows, :] = jnp.broadcast_to(m + jnp.log(l), (QB, LANES))
            return carry

        lax.fori_loop(0, dil * nb, body, 0, unroll=ATTN_UNROLL)

    def mix(c, carry):
        rows = pl.ds(pl.multiple_of(c * QB, QB), QB)
        l0 = lse_scr[0, rows, :]
        l1 = lse_scr[1, rows, :]
        l2 = lse_scr[2, rows, :]
        mx = jnp.maximum(jnp.maximum(l0, l1), l2)
        w0 = jnp.exp(l0 - mx)
        w1 = jnp.exp(l1 - mx)
        w2 = jnp.exp(l2 - mx)
        num = w0 * o_scr[0, rows, :] + w1 * o_scr[1, rows, :] + w2 * o_scr[2, rows, :]
        o_ref[rows, :] = (num / (w0 + w1 + w2)).astype(o_ref.dtype)
        return carry

    lax.fori_loop(0, SEQ // QB, mix, 0, unroll=4)


def _attn_prompt(h):
    col = lambda off: (lambda b, hh: (b, off // A_HEAD_DIM + hh))
    return pl.pallas_call(
        _attn_prompt_body,
        grid=(BATCH, A_HEADS),
        in_specs=[
            pl.BlockSpec((SEQ, A_HEAD_DIM), col(OFF_QA)),
            pl.BlockSpec((SEQ, A_HEAD_DIM), col(OFF_KA)),
            pl.BlockSpec((SEQ, A_HEAD_DIM), col(OFF_VA)),
        ],
        out_specs=pl.BlockSpec((SEQ, A_HEAD_DIM), lambda b, hh: (b, hh)),
        out_shape=jax.ShapeDtypeStruct((N_PROMPT, A_WIDTH), BF16),
        scratch_shapes=[
            pltpu.VMEM((len(DILATED_PATTERNS), SEQ, A_HEAD_DIM), F32),
            pltpu.VMEM((len(DILATED_PATTERNS), SEQ, LANES), F32),
        ],
        compiler_params=_cparams(("parallel", "parallel"), 32),
        name="attn_prompt",
    )(h, h, h)


def _rotate(x, cos, sin):
    half = x.shape[-1] // 2
    x1 = x[:, :half]
    x2 = x[:, half:]
    return jnp.concatenate([x1 * cos - x2 * sin, x1 * sin + x2 * cos], axis=-1)


def _group_norm_gate(o, g):
    mu = jnp.mean(o, axis=-1, keepdims=True)
    var = jnp.mean(jnp.square(o - mu), axis=-1, keepdims=True)
    on = (o - mu) * lax.rsqrt(var + LN_EPS)
    return on * (g * jax.nn.sigmoid(g))


def _ret_prompt_body(q_ref, k_ref, v_ref, g_ref, cos_ref, sin_ref, dmask_ref, qdec_ref, kdec_ref,
                     cdec_ref, o_ref, st_ref, st_scr):
    st_scr[...] = jnp.zeros_like(st_scr)
    dmask = dmask_ref[...]
    qdec = qdec_ref[...]
    kdec = kdec_ref[...]
    cdec = cdec_ref[...]

    def chunk(c, carry):
        rows = pl.ds(pl.multiple_of(c * R_CHUNK, R_CHUNK), R_CHUNK)
        cos = cos_ref[rows, :]
        sin = sin_ref[rows, :]
        q = _rotate(q_ref[rows, :], cos, sin)
        k = _rotate(k_ref[rows, :], cos, sin) * (R_QK_DIM ** -0.5)
        v = v_ref[rows, :].astype(BF16)
        st = st_scr[...]
        inner = _dot_nt(q.astype(BF16), k.astype(BF16)) * dmask
        o = _dot(inner.astype(BF16), v) + _dot((q * qdec).astype(BF16), st.astype(BF16))
        st_scr[...] = cdec * st + _dot_tn((k * kdec).astype(BF16), v)
        o_ref[rows, :] = _group_norm_gate(o, g_ref[rows, :]).astype(o_ref.dtype)
        return carry

    lax.fori_loop(0, SEQ // R_CHUNK, chunk, 0, unroll=4)
    st_ref[...] = st_scr[...]


def _retention_tables():
    lg = jnp.log1p(-jnp.exp2(-5.0 - jnp.arange(R_HEADS, dtype=F32)))
    i = jnp.arange(R_CHUNK, dtype=F32)
    rel = i[:, None] - i[None, :]
    dmask = jnp.exp(jnp.where(rel[None] >= 0, rel[None] * lg[:, None, None], -jnp.inf))
    qdec = jnp.exp((i[None, :] + 1.0) * lg[:, None])
    kdec = jnp.exp((R_CHUNK - 1.0 - i)[None, :] * lg[:, None])
    cdec = jnp.exp(R_CHUNK * lg)
    bcast = lambda t: jnp.broadcast_to(t[:, :, None], (R_HEADS, R_CHUNK, R_QK_DIM))
    return (dmask, bcast(qdec), bcast(kdec),
            jnp.broadcast_to(cdec[:, None, None], (R_HEADS, R_QK_DIM, R_V_DIM)), lg)


def _rope_tables(pos):
    half = R_QK_DIM // 2
    inv = 1.0 / (ROPE_BASE ** jnp.linspace(0.0, 1.0, half, dtype=F32))
    ang = pos[:, None] * inv[None, :]
    return jnp.cos(ang), jnp.sin(ang)


def _ret_prompt(h, tables, rope):
    dmask, qdec, kdec, cdec, _ = tables
    cos, sin = rope
    col = lambda off: (lambda b, hh: (b, off // R_QK_DIM + hh))
    per_head = lambda shape: pl.BlockSpec((None,) + shape, lambda b, hh: (hh, 0, 0))
    return pl.pallas_call(
        _ret_prompt_body,
        grid=(BATCH, R_HEADS),
        in_specs=[
            pl.BlockSpec((SEQ, R_QK_DIM), col(OFF_QR)),
            pl.BlockSpec((SEQ, R_QK_DIM), col(OFF_KR)),
            pl.BlockSpec((SEQ, R_V_DIM), col(OFF_VR)),
            pl.BlockSpec((SEQ, R_V_DIM), col(OFF_GR)),
            pl.BlockSpec((SEQ, R_QK_DIM // 2), lambda b, hh: (0, 0)),
            pl.BlockSpec((SEQ, R_QK_DIM // 2), lambda b, hh: (0, 0)),
            per_head((R_CHUNK, R_CHUNK)),
            per_head((R_CHUNK, R_QK_DIM)),
            per_head((R_CHUNK, R_QK_DIM)),
            per_head((R_QK_DIM, R_V_DIM)),
        ],
        out_specs=[
            pl.BlockSpec((SEQ, R_V_DIM), lambda b, hh: (b, hh)),
            pl.BlockSpec((None, None, R_QK_DIM, R_V_DIM), lambda b, hh: (b, hh, 0, 0)),
        ],
        out_shape=[
            jax.ShapeDtypeStruct((N_PROMPT, R_WIDTH), BF16),
            jax.ShapeDtypeStruct((BATCH, R_HEADS, R_QK_DIM, R_V_DIM), F32),
        ],
        scratch_shapes=[pltpu.VMEM((R_QK_DIM, R_V_DIM), F32)],
        compiler_params=_cparams(("parallel", "parallel"), 48),
        name="ret_prompt",
    )(h, h, h, h, cos, sin, dmask, qdec, kdec, cdec)


def _sample_attn_body(*refs, n_alias):
    q_ref, kn_ref, vn_ref, ck_ref, ckn_ref, cv_ref, cvn_ref = refs[:7]
    o_ref, ok_ref, ov_ref, m_scr, l_scr, acc_scr = refs[7 + n_alias:]
    c = pl.program_id(1)
    last = pl.num_programs(1) - 1
    scale = A_HEAD_DIM ** -0.5
    q = q_ref[...]

    @pl.when(c == 0)
    def _():
        m_scr[...] = jnp.full_like(m_scr, NEG)
        l_scr[...] = jnp.zeros_like(l_scr)
        acc_scr[...] = jnp.zeros_like(acc_scr)

    def accumulate(kr, vr, mult):
        s = jnp.sum(kr * q[None], axis=-1, keepdims=True) * scale
        m_old = m_scr[...]
        m_new = jnp.maximum(m_old, jnp.max(s, axis=0))
        alpha = jnp.exp(m_old - m_new)
        pr = jnp.exp(s - m_new[None])
        l_scr[...] = alpha * l_scr[...] + mult * jnp.sum(pr, axis=0)
        acc_scr[...] = alpha * acc_scr[...] + mult * jnp.sum(pr * vr, axis=0)
        m_scr[...] = m_new

    accumulate(ck_ref[pl.ds(0, SAMPLE_CH // 16, stride=16)], cv_ref[pl.ds(0, SAMPLE_CH // 16, stride=16)], 1.0)

    @pl.when(c == last)
    def _():
        d4 = pl.ds(SAMPLE_CH - 4 * QB, QB, stride=4)
        accumulate(ck_ref[d4], cv_ref[d4], 1.0)
        accumulate(ck_ref[pl.ds(SAMPLE_CH - QB, QB)], cv_ref[pl.ds(SAMPLE_CH - QB, QB)], 1.0)
        accumulate(kn_ref[...][None], vn_ref[...][None], float(len(DILATED_PATTERNS)))
        o_ref[...] = acc_scr[...] / l_scr[...]

    ok_ref[pl.ds(0, SAMPLE_CH - 1)] = ck_ref[pl.ds(1, SAMPLE_CH - 1)]
    ov_ref[pl.ds(0, SAMPLE_CH - 1)] = cv_ref[pl.ds(1, SAMPLE_CH - 1)]
    is_last = c == last
    ok_ref[SAMPLE_CH - 1] = jnp.where(is_last, kn_ref[...], ckn_ref[0])
    ov_ref[SAMPLE_CH - 1] = jnp.where(is_last, vn_ref[...], cvn_ref[0])


def _sample_attn(q_s, k_s, v_s, cache_k, cache_v, layer, prev_k=None, prev_v=None):
    assert PAST_LEN == W_MAX and W_MAX % SAMPLE_CH == 0 and SAMPLE_CH >= 4 * QB and SAMPLE_CH % 16 == 0
    nc = W_MAX // SAMPLE_CH
    row = pl.BlockSpec((None, A_HEADS, A_HEAD_DIM), lambda b, c: (b, 0, 0))
    chunk = pl.BlockSpec((None, None, SAMPLE_CH, A_HEADS, A_HEAD_DIM), lambda b, c: (layer, b, c, 0, 0))
    nxt = pl.BlockSpec((None, None, 1, A_HEADS, A_HEAD_DIM),
                       lambda b, c: (layer, b, jnp.minimum((c + 1) * SAMPLE_CH, W_MAX - 1), 0, 0))
    n_alias = 0 if prev_k is None else 2
    alias_specs = [pl.BlockSpec(memory_space=pl.ANY)] * n_alias
    alias_args = [] if prev_k is None else [prev_k, prev_v]
    win_shape = jax.ShapeDtypeStruct((DEPTH, DEC_BATCH, W_MAX, A_HEADS, A_HEAD_DIM), F32)
    return pl.pallas_call(
        functools.partial(_sample_attn_body, n_alias=n_alias),
        grid=(DEC_BATCH, nc),
        in_specs=[row, row, row, chunk, nxt, chunk, nxt] + alias_specs,
        out_specs=[row, chunk, chunk],
        out_shape=[jax.ShapeDtypeStruct((DEC_BATCH, A_HEADS, A_HEAD_DIM), F32), win_shape, win_shape],
        scratch_shapes=[pltpu.VMEM((A_HEADS, A_HEAD_DIM), F32)] * 3,
        input_output_aliases={7: 1, 8: 2} if n_alias else {},
        compiler_params=_cparams(("parallel", "arbitrary"), 48),
        name="sample_attn",
    )(q_s, k_s, v_s, cache_k, cache_k, cache_v, cache_v, *alias_args)


def _sample_ret_body(*refs, n_alias, gammas):
    q_ref, k_ref, v_ref, g_ref, cos_ref, sin_ref, eye_ref, st_ref = refs[:8]
    o_ref, sto_ref = refs[8 + n_alias:]
    cos = cos_ref[...]
    sin = sin_ref[...]
    eye = eye_ref[...]
    for bb in range(SAMPLE_RET_BB):
        outs = []
        for hh in range(R_HEADS):
            cols = slice(hh * R_QK_DIM, (hh + 1) * R_QK_DIM)
            q = _rotate(q_ref[bb, :, cols], cos, sin)
            k = _rotate(k_ref[bb, :, cols], cos, sin) * (R_QK_DIM ** -0.5)
            v = v_ref[bb, :, cols]
            st = st_ref[bb, hh]
            inner = jnp.sum(q * k, axis=-1, keepdims=True)
            q_dec = jnp.broadcast_to(q * gammas[hh], (8, R_QK_DIM)).astype(BF16)
            o = inner * v + _dot(q_dec, st.astype(BF16))[0:1]
            k_col = _dot_nt(eye, jnp.broadcast_to(k, (R_QK_DIM, R_QK_DIM)).astype(BF16))
            sto_ref[bb, hh] = gammas[hh] * st + k_col * v
            outs.append(_group_norm_gate(o, g_ref[bb, :, cols]))
        o_ref[bb] = jnp.concatenate(outs, axis=-1)


def _sample_ret(q_s, k_s, v_s, g_s, state, layer, rope_s, eye, gammas, prev=None):
    row = lambda w: pl.BlockSpec((SAMPLE_RET_BB, 1, w), lambda b: (b, 0, 0))
    const = lambda shape: pl.BlockSpec(shape, lambda b: (0,) * len(shape))
    st_spec = pl.BlockSpec((None, SAMPLE_RET_BB, R_HEADS, R_QK_DIM, R_V_DIM), lambda b: (layer, b, 0, 0, 0))
    n_alias = 0 if prev is None else 1
    return pl.pallas_call(
        functools.partial(_sample_ret_body, n_alias=n_alias, gammas=gammas),
        grid=(DEC_BATCH // SAMPLE_RET_BB,),
        in_specs=[row(R_HEADS * R_QK_DIM), row(R_HEADS * R_QK_DIM), row(R_WIDTH), row(R_WIDTH),
                  const((1, R_QK_DIM // 2)), const((1, R_QK_DIM // 2)), const((R_QK_DIM, R_QK_DIM)), st_spec]
        + [pl.BlockSpec(memory_space=pl.ANY)] * n_alias,
        out_specs=[row(R_WIDTH), st_spec],
        out_shape=[jax.ShapeDtypeStruct((DEC_BATCH, 1, R_WIDTH), F32),
                   jax.ShapeDtypeStruct((DEPTH, DEC_BATCH, R_HEADS, R_QK_DIM, R_V_DIM), F32)],
        input_output_aliases={8: 1} if n_alias else {},
        compiler_params=_cparams(("parallel",), 32),
        name="sample_ret",
    )(q_s, k_s, v_s, g_s, rope_s[0], rope_s[1], eye, state, *([] if prev is None else [prev]))


def _layer_norm(y, g, b):
    mu = jnp.mean(y, axis=-1, keepdims=True)
    var = jnp.mean(jnp.square(y - mu), axis=-1, keepdims=True)
    return (y - mu) * lax.rsqrt(var + LN_EPS) * g + b


def _split_bf16(x):
    hi = x.astype(BF16)
    lo = (x - hi.astype(F32)).astype(BF16)
    return hi, lo


def _outproj_body(attn_ref, ret_ref, x_ref, w_ref, g_ref, b_ref, wr_ref, br_ref,
                  h_ref, rdst_ref, rw_ref, cnt_ref, run_scr):
    @pl.when(pl.program_id(0) == 0)
    def _():
        run_scr[...] = jnp.zeros_like(run_scr)

    acc = _dot(attn_ref[...], w_ref[0:A_WIDTH, :]) + _dot(ret_ref[...], w_ref[A_WIDTH:, :])
    h = _layer_norm(ALPHA * x_ref[...] + acc, g_ref[...], b_ref[...])
    h_ref[...] = h

    h_hi, h_lo = _split_bf16(h)
    w_hi, w_lo = _split_bf16(wr_ref[...])
    logits = _dot(h_hi, w_hi) + _dot(h_hi, w_lo) + _dot(h_lo, w_hi) + br_ref[...]
    lane = lax.broadcasted_iota(jnp.int32, logits.shape, 1)
    valid = lane < N_EXPERTS
    logits = jnp.where(valid, logits, NEG)
    e = jnp.exp(logits - jnp.max(logits, axis=-1, keepdims=True))
    probs = e / jnp.sum(e, axis=-1, keepdims=True)
    grp = lane // EXPERTS_PER_GROUP
    best = jnp.max(jnp.where(grp == 0, probs, -1.0), axis=-1, keepdims=True)
    sel = jnp.zeros_like(best, dtype=jnp.int32)
    for gi in range(1, N_GROUPS):
        gm = jnp.max(jnp.where(grp == gi, probs, -1.0), axis=-1, keepdims=True)
        better = gm > best
        sel = jnp.where(better, gi, sel)
        best = jnp.maximum(best, gm)
    pv = jnp.where(grp == sel, probs, -1.0)
    v1 = jnp.max(pv, axis=-1, keepdims=True)
    i1 = jnp.min(jnp.where(pv == v1, lane, LANES), axis=-1, keepdims=True)
    pv2 = jnp.where(lane == i1, -1.0, pv)
    v2 = jnp.max(pv2, axis=-1, keepdims=True)
    i2 = jnp.min(jnp.where(pv2 == v2, lane, LANES), axis=-1, keepdims=True)
    tot = v1 + v2
    rw_ref[...] = jnp.where(lane == 0, v1 / tot, jnp.where(lane == 1, v2 / tot, 0.0))

    pick1 = lane == i1
    pick2 = lane == i2
    member = jnp.where(pick1 | pick2, 1.0, 0.0)
    ri = lax.broadcasted_iota(jnp.int32, (ROW_TM, ROW_TM), 0)
    ci = lax.broadcasted_iota(jnp.int32, (ROW_TM, ROW_TM), 1)
    earlier = jnp.where(ci < ri, 1.0, 0.0).astype(BF16)
    run = run_scr[...]
    rank = _dot(earlier, member.astype(BF16)) + run
    rank1 = jnp.sum(jnp.where(pick1, rank, 0.0), axis=-1, keepdims=True).astype(jnp.int32)
    rank2 = jnp.sum(jnp.where(pick2, rank, 0.0), axis=-1, keepdims=True).astype(jnp.int32)
    rdst_ref[...] = jnp.where(lane == 0, i1 * MOE_CAP + rank1, jnp.where(lane == 1, i2 * MOE_CAP + rank2, 0))
    run = run + jnp.sum(member, axis=0, keepdims=True)
    run_scr[...] = run
    cnt_ref[...] = jnp.broadcast_to(run, cnt_ref.shape).astype(jnp.int32)


def _outproj(attn, ret, x, w_out_b, ln_g, ln_b, w_router_p, b_router_p, layer):
    rows = lambda w: pl.BlockSpec((ROW_TM, w), lambda i: (i, 0))
    vec = lambda: pl.BlockSpec((None, 1, D_MODEL), lambda i: (layer, 0, 0))
    return pl.pallas_call(
        _outproj_body,
        grid=(N_TOK // ROW_TM,),
        in_specs=[rows(A_WIDTH), rows(R_WIDTH), rows(D_MODEL),
                  pl.BlockSpec((None, A_WIDTH + R_WIDTH, D_MODEL), lambda i: (layer, 0, 0)),
                  vec(), vec(),
                  pl.BlockSpec((D_MODEL, LANES), lambda i: (0, 0)),
                  pl.BlockSpec((1, LANES), lambda i: (0, 0))],
        out_specs=[rows(D_MODEL), rows(LANES), rows(LANES), pl.BlockSpec((8, LANES), lambda i: (0, 0))],
        out_shape=[jax.ShapeDtypeStruct((N_TOK, D_MODEL), F32),
                   jax.ShapeDtypeStruct((N_TOK, LANES), jnp.int32),
                   jax.ShapeDtypeStruct((N_TOK, LANES), F32),
                   jax.ShapeDtypeStruct((8, LANES), jnp.int32)],
        scratch_shapes=[pltpu.VMEM((1, LANES), F32)],
        compiler_params=_cparams(("arbitrary",), 48),
        name="outproj_ln_router",
    )(attn, ret, x, w_out_b, ln_g, ln_b, w_router_p, b_router_p)


def _block_map(counts):
    nblk = (counts + MOE_RB - 1) // MOE_RB
    cum = jnp.cumsum(nblk)
    n_used = cum[-1]
    j = jnp.minimum(jnp.arange(MOE_NB, dtype=jnp.int32), n_used - 1)
    e = jnp.minimum(jnp.sum(j[:, None] >= cum[None, :], axis=1), N_EXPERTS - 1).astype(jnp.int32)
    first = (cum - nblk)[e]
    blk_row = e * MOE_CAP_BLOCKS + (j - first)
    return blk_row.astype(jnp.int32), e, n_used.astype(jnp.int32).reshape(1)


def _row_copy(src, src_row, dst, dst_row, sem):
    return pltpu.make_async_copy(src.at[pl.ds(src_row, 1), :], dst.at[pl.ds(dst_row, 1), :], sem)


def _dispatch_body(dst_ref, cnt_ref, h_ref, xs_hbm, zbuf, zsem, sem):
    i = pl.program_id(0)

    @pl.when(i == 0)
    def _():
        zbuf[...] = jnp.zeros_like(zbuf)

        def tail_copy(e):
            nblk = (cnt_ref[e] + MOE_RB - 1) // MOE_RB
            row0 = pl.multiple_of((e * MOE_CAP_BLOCKS + nblk - 1) * MOE_RB, MOE_RB)
            return pltpu.make_async_copy(zbuf, xs_hbm.at[pl.ds(row0, MOE_RB), :], zsem)

        for e in range(N_EXPERTS):
            @pl.when(cnt_ref[e] > 0)
            def _(e=e):
                tail_copy(e).start()
        for e in range(N_EXPERTS):
            @pl.when(cnt_ref[e] > 0)
            def _(e=e):
                tail_copy(e).wait()

    base = i * ROW_TM

    def scatter_row(r, carry):
        a = TOP_K * (base + r)
        _row_copy(h_ref, r, xs_hbm, dst_ref[a], sem).start()
        _row_copy(h_ref, r, xs_hbm, dst_ref[a + 1], sem).start()
        return carry

    lax.fori_loop(0, ROW_TM, scatter_row, 0, unroll=DMA_UNROLL)
    step_rows = xs_hbm.at[pl.ds(0, TOP_K * ROW_TM), :]
    pltpu.make_async_copy(step_rows, step_rows, sem).wait()


def _dispatch(h, dst, counts):
    return pl.pallas_call(
        _dispatch_body,
        grid_spec=pltpu.PrefetchScalarGridSpec(
            num_scalar_prefetch=2,
            grid=(N_TOK // ROW_TM,),
            in_specs=[pl.BlockSpec((ROW_TM, D_MODEL), lambda i, ds, ct: (i, 0))],
            out_specs=pl.BlockSpec(memory_space=pl.ANY),
            scratch_shapes=[pltpu.VMEM((MOE_RB, D_MODEL), F32),
                            pltpu.SemaphoreType.DMA(()), pltpu.SemaphoreType.DMA(())],
        ),
        out_shape=jax.ShapeDtypeStruct((N_EXPERTS * MOE_CAP, D_MODEL), F32),
        compiler_params=_cparams(("arbitrary",), 16),
        name="moe_dispatch",
    )(dst, counts, h)


def _experts_body(br_ref, be_ref, nu_ref, x_ref, wgu_ref, wd_ref, y_ref):
    @pl.when(pl.program_id(0) < nu_ref[0])
    def _():
        gu = _dot(x_ref[...].astype(BF16), wgu_ref[...])
        gate = gu[:, :D_EXPERT]
        act = gate * jax.nn.sigmoid(gate) * gu[:, D_EXPERT:]
        y_ref[...] = _dot(act.astype(BF16), wd_ref[...])


def _experts(xs, block_map, w_gate_up_b, w_down_b, layer):
    blk_row, block_e, n_used = block_map
    rows = pl.BlockSpec((MOE_RB, D_MODEL), lambda j, br, be, nu: (br[j], 0))
    return pl.pallas_call(
        _experts_body,
        grid_spec=pltpu.PrefetchScalarGridSpec(
            num_scalar_prefetch=3,
            grid=(MOE_NB,),
            in_specs=[
                rows,
                pl.BlockSpec((None, None, D_MODEL, 2 * D_EXPERT), lambda j, br, be, nu: (layer, be[j], 0, 0)),
                pl.BlockSpec((None, None, D_EXPERT, D_MODEL), lambda j, br, be, nu: (layer, be[j], 0, 0)),
            ],
            out_specs=rows,
        ),
        out_shape=jax.ShapeDtypeStruct((N_EXPERTS * MOE_CAP, D_MODEL), F32),
        compiler_params=_cparams(("arbitrary",), 48),
        name="moe_experts",
    )(blk_row, block_e, n_used, xs, w_gate_up_b, w_down_b)


def _post_body(dst_ref, h_ref, rw_ref, p_ref, wg_ref, wp_ref, g_ref, b_ref, bg_ref, ys_hbm,
               x_ref, xb_ref, ybuf, sem):
    base = pl.program_id(0) * ROW_TM

    def gather_row(r, carry):
        a = TOP_K * (base + r)
        _row_copy(ys_hbm, dst_ref[a], ybuf, r, sem).start()
        _row_copy(ys_hbm, dst_ref[a + 1], ybuf, ROW_TM + r, sem).start()
        return carry

    lax.fori_loop(0, ROW_TM, gather_row, 0, unroll=DMA_UNROLL)
    pe = _dot(p_ref[...].astype(BF16), wp_ref[...].astype(BF16))
    pltpu.make_async_copy(ys_hbm.at[pl.ds(0, TOP_K * ROW_TM), :], ybuf, sem).wait()

    rw = rw_ref[...]
    ff = ybuf[0:ROW_TM, :] * rw[:, 0:1] + ybuf[ROW_TM:, :] * rw[:, 1:2]
    h = _layer_norm(ALPHA * h_ref[...] + ff, g_ref[...], b_ref[...])
    gate = jax.nn.sigmoid(_dot(h.astype(BF16), wg_ref[...]) + bg_ref[...])
    x = h + gate * pe
    x_ref[...] = x
    xb_ref[...] = x.astype(BF16)


def _post(h, ys, dst, rw, p, w_ple_gate_b, w_ple_proj, ln_g, ln_b, b_gate, layer):
    rows = lambda w: pl.BlockSpec((ROW_TM, w), lambda i, ds: (i, 0))
    vec = lambda: pl.BlockSpec((None, 1, D_MODEL), lambda i, ds: (layer, 0, 0))
    return pl.pallas_call(
        _post_body,
        grid_spec=pltpu.PrefetchScalarGridSpec(
            num_scalar_prefetch=1,
            grid=(N_TOK // ROW_TM,),
            in_specs=[rows(D_MODEL), rows(LANES),
                      pl.BlockSpec((None, ROW_TM, PLE_DIM), lambda i, ds: (layer, i, 0)),
                      pl.BlockSpec((None, D_MODEL, D_MODEL), lambda i, ds: (layer, 0, 0)),
                      pl.BlockSpec((None, PLE_DIM, D_MODEL), lambda i, ds: (layer, 0, 0)),
                      vec(), vec(), vec(),
                      pl.BlockSpec(memory_space=pl.ANY)],
            out_specs=[rows(D_MODEL), rows(D_MODEL)],
            scratch_shapes=[pltpu.VMEM((TOP_K * ROW_TM, D_MODEL), F32), pltpu.SemaphoreType.DMA(())],
        ),
        out_shape=[jax.ShapeDtypeStruct((N_TOK, D_MODEL), F32),
                   jax.ShapeDtypeStruct((N_TOK, D_MODEL), BF16)],
        compiler_params=_cparams(("arbitrary",), 56),
        name="combine_ln_ple",
    )(dst, h, rw, p, w_ple_gate_b, w_ple_proj, ln_g, ln_b, b_gate, ys)


def kernel(x_prompt, x_sample, cache_win_k, cache_win_v, state_ret, p_prompt, p_sample,
           w_in, w_out, ln1_g, ln1_b, ln2_g, ln2_b, w_router, b_router,
           w_gate_up, w_down, w_ple_proj, w_ple_gate, b_ple_gate):
    x = jnp.concatenate([x_prompt.reshape(N_PROMPT, D_MODEL), x_sample.reshape(DEC_BATCH, D_MODEL)], axis=0)
    xb = x.astype(BF16)
    p_all = jnp.concatenate([p_prompt.reshape(DEPTH, N_PROMPT, PLE_DIM),
                             p_sample.reshape(DEPTH, DEC_BATCH, PLE_DIM)], axis=1)
    w_out_b = w_out.astype(BF16)
    w_gate_up_b = w_gate_up.astype(BF16)
    w_down_b = w_down.astype(BF16)
    w_ple_gate_b = w_ple_gate.astype(BF16)
    w_router_p = jnp.pad(w_router, ((0, 0), (0, LANES - N_EXPERTS)))
    b_router_p = jnp.pad(b_router, (0, LANES - N_EXPERTS)).reshape(1, LANES)
    vec3 = lambda t: t.reshape(DEPTH, 1, D_MODEL)
    ln1_g, ln1_b, ln2_g, ln2_b, b_ple_gate = map(vec3, (ln1_g, ln1_b, ln2_g, ln2_b, b_ple_gate))

    tables = _retention_tables()
    gammas = tuple(float(1.0 - 2.0 ** (-5.0 - hh)) for hh in range(R_HEADS))
    rope_p = _rope_tables(jnp.arange(SEQ, dtype=F32))
    rope_s = _rope_tables(PAST_LEN + jnp.arange(1, dtype=F32))
    eye = jnp.eye(R_QK_DIM, dtype=BF16)

    win_k = win_v = st_s = None
    pk, pv, ps = [], [], []
    for layer in range(DEPTH):
        h = _inproj(xb, w_in, layer)
        attn_p = _attn_prompt(h)
        ret_p, st_p = _ret_prompt(h, tables, rope_p)
        hs = h[N_PROMPT:]
        heads = lambda off: hs[:, off:off + A_WIDTH].reshape(DEC_BATCH, A_HEADS, A_HEAD_DIM)
        attn_s, win_k, win_v = _sample_attn(heads(OFF_QA), heads(OFF_KA), heads(OFF_VA),
                                            cache_win_k, cache_win_v, layer, win_k, win_v)
        wide = lambda off, w: hs[:, off:off + w].reshape(DEC_BATCH, 1, w)
        ret_s, st_s = _sample_ret(wide(OFF_QR, R_HEADS * R_QK_DIM), wide(OFF_KR, R_HEADS * R_QK_DIM),
                                  wide(OFF_VR, R_WIDTH), wide(OFF_GR, R_WIDTH),
                                  state_ret, layer, rope_s, eye, gammas, st_s)
        attn = jnp.concatenate([attn_p, attn_s.reshape(DEC_BATCH, A_WIDTH).astype(BF16)], axis=0)
        ret = jnp.concatenate([ret_p, ret_s.reshape(DEC_BATCH, R_WIDTH).astype(BF16)], axis=0)
        h1, rdst, rw, cnt = _outproj(attn, ret, x, w_out_b, ln1_g, ln1_b, w_router_p, b_router_p, layer)
        dst = rdst[:, :TOP_K].reshape(N_ASSIGN)
        counts = cnt[0, :N_EXPERTS]
        xs = _dispatch(h1, dst, counts)
        ys = _experts(xs, _block_map(counts), w_gate_up_b, w_down_b, layer)
        x, xb = _post(h1, ys, dst, rw, p_all, w_ple_gate_b, w_ple_proj, ln2_g, ln2_b, b_ple_gate, layer)
        pk.append(h[:N_PROMPT, OFF_KA:OFF_KA + A_WIDTH].reshape(BATCH, SEQ, A_HEADS, A_HEAD_DIM))
        pv.append(h[:N_PROMPT, OFF_VA:OFF_VA + A_WIDTH].reshape(BATCH, SEQ, A_HEADS, A_HEAD_DIM))
        ps.append(st_p)

    y_prompt = x[:N_PROMPT].reshape(BATCH, SEQ, D_MODEL)
    y_sample = x[N_PROMPT:].reshape(DEC_BATCH, 1, D_MODEL)
    return (y_prompt, y_sample, jnp.stack(pk), jnp.stack(pv), jnp.stack(ps), win_k, win_v, st_s)
```
